```python
import numpy as np
import jax, jax.numpy as jnp
from jax import lax

D_MODEL = 1024
BATCH = 2
SEQ = 8192
DEPTH = 4

HEAD_DIM = 64
NSA_HEADS = 6
NSA_KV_HEADS = 2
NSA_GROUP = NSA_HEADS // NSA_KV_HEADS
RWKV_HEADS = 4
MOBA_HEADS = 6
D_NSA = NSA_HEADS * HEAD_DIM
D_NSA_KV = NSA_KV_HEADS * HEAD_DIM
D_RWKV = RWKV_HEADS * HEAD_DIM
D_MOBA = MOBA_HEADS * HEAD_DIM
D_MIX = D_NSA + D_RWKV + D_MOBA

CMP_LEN = 32
CMP_STRIDE = 16
CMP_HIDDEN = 256
SLC_BLOCK = 64
SLC_TOPK = 16
WINDOW = 512
N_BRANCH = 3
RW_DECAY_LORA = 32
RW_AAA_LORA = 32
RW_GATE_LORA = 64
RW_DECAY_SCALE = 0.606531
RW_LN_EPS = 64e-5
MOBA_BLOCK = 256
MOBA_TOPK = 3
D_FF = 2816
CONV_WIDTH = 3
Q_CHUNK = 128
NORM_EPS = 1e-6
BIG = 1e9

N_IN_NSA = D_NSA + 6 * D_NSA_KV + NSA_HEADS * N_BRANCH
N_IN_RWKV = 3 * D_RWKV + RW_DECAY_LORA + RW_AAA_LORA + RW_GATE_LORA
N_IN_MOBA = 3 * D_MOBA
N_IN = N_IN_NSA + N_IN_RWKV + N_IN_MOBA

kernel_name = 'hymba_nsa_rwkv7_moba_convglu'


def _rmsnorm(x, g):
    xf = x.astype(jnp.float32)
    y = xf * lax.rsqrt(jnp.mean(xf * xf, axis=-1, keepdims=True) + NORM_EPS)
    return (y * g.astype(jnp.float32)).astype(x.dtype)


def _head_rmsnorm(o, g):
    b, s, d = o.shape
    oh = o.reshape(b, s, d // HEAD_DIM, HEAD_DIM)
    return _rmsnorm(oh, g.reshape(d // HEAD_DIM, HEAD_DIM)).reshape(b, s, d)


def _masked_softmax(s, mask):
    s = jnp.where(mask, s.astype(jnp.float32), -jnp.inf)
    m = jnp.max(s, axis=-1, keepdims=True)
    m = jnp.where(jnp.isfinite(m), m, 0.0)
    p = jnp.exp(s - m)
    return p / jnp.maximum(jnp.sum(p, axis=-1, keepdims=True), 1e-30)


def _split(a, sizes):
    return jnp.split(a, [int(i) for i in np.cumsum(sizes)[:-1]], axis=-1)


def _nsa_compress(kv, pos, w1, w2):
    b, s, g, d = kv.shape
    n_cmp = (s - CMP_LEN) // CMP_STRIDE + 1
    idx = np.arange(n_cmp)[:, None] * CMP_STRIDE + np.arange(CMP_LEN)[None, :]
    blk = kv[:, idx] + pos[:, None, :]
    blk = jnp.transpose(blk, (0, 3, 1, 2, 4)).reshape(b, g, n_cmp, CMP_LEN * d)
    return jax.nn.gelu(blk @ w1) @ w2


def _cmp_to_slc(n_cmp, n_slc):
    r = SLC_BLOCK // CMP_STRIDE
    c = CMP_LEN // CMP_STRIDE
    i = (r * np.arange(n_slc)[:, None, None] - np.arange(r)[None, :, None] - np.arange(c)[None, None, :]).reshape(n_slc, -1)
    m = (i[:, :, None] == np.arange(n_cmp)[None, None, :]).sum(1)
    return jnp.asarray(m.T, jnp.float32)


def _nsa_mixer(p, cmp_pos, cmp_w1, cmp_w2):
    b, s, _ = p.shape
    q, kc, vc, ks, vs, kw, vw, gt = _split(p, [D_NSA] + [D_NSA_KV] * 6 + [NSA_HEADS * N_BRANCH])
    kvh = lambda t: t.reshape(b, s, NSA_KV_HEADS, HEAD_DIM)
    hf = lambda t: jnp.transpose(kvh(t), (0, 2, 1, 3))
    q = jnp.transpose(q.reshape(b, s, NSA_KV_HEADS, NSA_GROUP, HEAD_DIM), (0, 2, 3, 1, 4))
    gt = jnp.transpose(jax.nn.sigmoid(gt.reshape(b, s, NSA_KV_HEADS, NSA_GROUP, N_BRANCH)), (0, 2, 3, 1, 4))
    kc = _nsa_compress(kvh(kc), cmp_pos[0], cmp_w1[0], cmp_w2[0])
    vc = _nsa_compress(kvh(vc), cmp_pos[1], cmp_w1[1], cmp_w2[1])
    n_cmp = kc.shape[2]
    n_slc = s // SLC_BLOCK
    k_slc = min(SLC_TOPK, n_slc)
    ks = hf(ks).reshape(b, NSA_KV_HEADS, n_slc, SLC_BLOCK, HEAD_DIM)
    vs = hf(vs).reshape(b, NSA_KV_HEADS, n_slc, SLC_BLOCK, HEAD_DIM)
    kw = jnp.pad(hf(kw), ((0, 0), (0, 0), (WINDOW, 0), (0, 0)))
    vw = jnp.pad(hf(vw), ((0, 0), (0, 0), (WINDOW, 0), (0, 0)))
    cmp_map = _cmp_to_slc(n_cmp, n_slc)
    cmp_end = jnp.asarray(np.arange(n_cmp) * CMP_STRIDE + CMP_LEN - 1)
    blk_id = jnp.arange(n_slc)
    bi = jnp.arange(b)[:, None, None, None]
    gi = jnp.arange(NSA_KV_HEADS)[None, :, None, None]
    scale = HEAD_DIM ** -0.5

    def chunk(c):
        t0 = c * Q_CHUNK
        tpos = t0 + jnp.arange(Q_CHUNK)
        qc = lax.dynamic_slice_in_dim(q, t0, Q_CHUNK, axis=3)
        gc = lax.dynamic_slice_in_dim(gt, t0, Q_CHUNK, axis=3)
        pc = _masked_softmax(jnp.einsum('bgrqd,bgnd->bgrqn', qc, kc) * scale, cmp_end[None, :] <= tpos[:, None])
        oc = jnp.einsum('bgrqn,bgnd->bgrqd', pc.astype(vc.dtype), vc)
        imp = jnp.einsum('bgrqn,nj->bgqj', pc, cmp_map)
        cur = tpos // SLC_BLOCK
        forced = (blk_id[None, :] == 0) | (blk_id[None, :] == cur[:, None]) | (blk_id[None, :] == cur[:, None] - 1)
        causal = blk_id[None, :] * SLC_BLOCK <= tpos[:, None]
        imp = jnp.where(forced, BIG, jnp.where(causal, imp, -BIG))
        _, sel = lax.top_k(imp, k_slc)
        kg = ks[bi, gi, sel].reshape(b, NSA_KV_HEADS, Q_CHUNK, k_slc * SLC_BLOCK, HEAD_DIM)
        vg = vs[bi, gi, sel].reshape(b, NSA_KV_HEADS, Q_CHUNK, k_slc * SLC_BLOCK, HEAD_DIM)
        kpos = (sel[..., None] * SLC_BLOCK + jnp.arange(SLC_BLOCK)).reshape(b, NSA_KV_HEADS, Q_CHUNK, k_slc * SLC_BLOCK)
        ps = _masked_softmax(jnp.einsum('bgrqd,bgqkd->bgrqk', qc, kg) * scale, (kpos <= tpos[:, None])[:, :, None])
        osl = jnp.einsum('bgrqk,bgqkd->bgrqd', ps.astype(vg.dtype), vg)
        kwc = lax.dynamic_slice_in_dim(kw, t0, Q_CHUNK + WINDOW, axis=2)
        vwc = lax.dynamic_slice_in_dim(vw, t0, Q_CHUNK + WINDOW, axis=2)
        wpos = t0 - WINDOW + jnp.arange(Q_CHUNK + WINDOW)
        dist = tpos[:, None] - wpos[None, :]
        mw = (dist >= 0) & (dist < WINDOW) & (wpos[None, :] >= 0)
        pw = _masked_softmax(jnp.einsum('bgrqd,bgkd->bgrqk', qc, kwc) * scale, mw)
        ow = jnp.einsum('bgrqk,bgkd->bgrqd', pw.astype(vwc.dtype), vwc)
        return gc[..., 0:1] * oc + gc[..., 1:2] * osl + gc[..., 2:3] * ow

    o = lax.map(chunk, jnp.arange(s // Q_CHUNK))
    return jnp.transpose(o, (1, 0, 4, 2, 3, 5)).reshape(b, s, D_NSA)


def _rwkv7_mixer(p, mu, w0, w2, a0, a2, g2, k_k, k_a, r_k, lnx_w, lnx_b):
    b, s, _ = p.shape
    f32 = jnp.float32
    p_prev = jnp.pad(p, ((0, 0), (1, 0), (0, 0)))[:, :-1]
    p = p + (p_prev - p) * mu
    r, k, v, xw, xa, xg = _split(p, [D_RWKV] * 3 + [RW_DECAY_LORA, RW_AAA_LORA, RW_GATE_LORA])
    w = jnp.exp(-RW_DECAY_SCALE * jax.nn.sigmoid((w0 + jnp.tanh(xw) @ w2).astype(f32)))
    a = jax.nn.sigmoid(a0 + xa @ a2)
    g = jax.nn.sigmoid(xg) @ g2
    heads = lambda t: t.reshape(b, s, RWKV_HEADS, HEAD_DIM).astype(f32)
    kk = heads(k * k_k)
    kk = kk / jnp.maximum(jnp.sqrt(jnp.sum(kk * kk, axis=-1, keepdims=True)), 1e-12)
    k = k * (1.0 + (a - 1.0) * k_a)
    r_h, w_h, k_h, v_h, a_h = heads(r), heads(w), heads(k), heads(v), heads(a)

    def step(state, inp):
        r_t, w_t, k_t, v_t, kk_t, a_t = inp
        sa = jnp.einsum('bhvk,bhk->bhv', state, -kk_t)
        state = state * w_t[:, :, None, :] + sa[..., None] * (kk_t * a_t)[:, :, None, :] + v_t[..., None] * k_t[:, :, None, :]
        return state, jnp.einsum('bhvk,bhk->bhv', state, r_t)

    xs = tuple(jnp.moveaxis(t, 1, 0) for t in (r_h, w_h, k_h, v_h, kk, a_h))
    _, out = lax.scan(step, jnp.zeros((b, RWKV_HEADS, HEAD_DIM, HEAD_DIM), f32), xs)
    out = jnp.moveaxis(out, 0, 1)
    mean = jnp.mean(out, axis=-1, keepdims=True)
    var = jnp.mean(jnp.square(out - mean), axis=-1, keepdims=True)
    out = ((out - mean) * lax.rsqrt(var + RW_LN_EPS)).reshape(b, s, D_RWKV) * lnx_w + lnx_b
    bonus = jnp.sum(r_h * k_h * r_k, axis=-1, keepdims=True) * v_h
    out = (out + bonus.reshape(b, s, D_RWKV)) * g
    return out.astype(p.dtype)


def _moba_mixer(p):
    b, s, _ = p.shape
    q, k, v = [jnp.transpose(t.reshape(b, s, MOBA_HEADS, HEAD_DIM), (0, 2, 1, 3)) for t in _split(p, [D_MOBA] * 3)]
    n_blk = -(-s // MOBA_BLOCK)
    pad = n_blk * MOBA_BLOCK - s
    k = jnp.pad(k, ((0, 0), (0, 0), (0, pad), (0, 0)))
    v = jnp.pad(v, ((0, 0), (0, 0), (0, pad), (0, 0)))
    kb = k.reshape(b, MOBA_HEADS, n_blk, MOBA_BLOCK, HEAD_DIM)
    vb = v.reshape(b, MOBA_HEADS, n_blk, MOBA_BLOCK, HEAD_DIM)
    kmean = jnp.mean(kb.astype(jnp.float32), axis=3)
    n_sel = min(MOBA_TOPK, max(n_blk - 1, 1))
    n_g = n_sel * MOBA_BLOCK
    blk_id = jnp.arange(n_blk)
    bi = jnp.arange(b)[:, None, None, None]
    hi = jnp.arange(MOBA_HEADS)[None, :, None, None]
    scale = HEAD_DIM ** -0.5

    def chunk(c):
        t0 = c * Q_CHUNK
        tpos = t0 + jnp.arange(Q_CHUNK)
        cur = t0 // MOBA_BLOCK
        qc = lax.dynamic_slice_in_dim(q, t0, Q_CHUNK, axis=2)
        gate = jnp.einsum('bhqd,bhnd->bhqn', qc.astype(jnp.float32), kmean)
        gate = jnp.where(blk_id < cur, gate, -jnp.inf)
        _, sel = lax.top_k(gate, n_sel)
        kg = kb[bi, hi, sel].reshape(b, MOBA_HEADS, Q_CHUNK, n_g, HEAD_DIM)
        vg = vb[bi, hi, sel].reshape(b, MOBA_HEADS, Q_CHUNK, n_g, HEAD_DIM)
        mg = jnp.repeat(sel < cur, MOBA_BLOCK, axis=-1)
        ko = lax.dynamic_slice_in_dim(k, cur * MOBA_BLOCK, MOBA_BLOCK, axis=2)
        vo = lax.dynamic_slice_in_dim(v, cur * MOBA_BLOCK, MOBA_BLOCK, axis=2)
        mo = (cur * MOBA_BLOCK + jnp.arange(MOBA_BLOCK))[None, :] <= tpos[:, None]
        s_g = jnp.einsum('bhqd,bhqkd->bhqk', qc, kg)
        s_o = jnp.einsum('bhqd,bhkd->bhqk', qc, ko)
        mask = jnp.concatenate([mg, jnp.broadcast_to(mo, s_o.shape)], axis=-1)
        prob = _masked_softmax(jnp.concatenate([s_g, s_o], axis=-1) * scale, mask).astype(v.dtype)
        return jnp.einsum('bhqk,bhqkd->bhqd', prob[..., :n_g], vg) + jnp.einsum('bhqk,bhkd->bhqd', prob[..., n_g:], vo)

    o = lax.map(chunk, jnp.arange(s // Q_CHUNK))
    return jnp.transpose(o, (1, 0, 3, 2, 4)).reshape(b, s, D_MOBA)


def _conv_glu(h, w_in, conv_w, conv_b, w_out):
    u, gt = jnp.split(h @ w_in, 2, axis=-1)
    gt = lax.conv_general_dilated(gt, conv_w[:, None, :], window_strides=(1,), padding=[(CONV_WIDTH - 1, 0)], dimension_numbers=('NWC', 'WIO', 'NWC'), feature_group_count=D_FF) + conv_b
    return (jax.nn.silu(gt) * u) @ w_out


def setup_inputs(seed: int = 0) -> dict:
    key = jax.random.key(seed)
    ks = iter(jax.random.split(key, 32))
    nrm = lambda shape, sc: jax.random.normal(next(ks), shape, jnp.float32) * sc
    gain = lambda shape: 1.0 + nrm(shape, 0.02)
    L = DEPTH
    resid = (2 * DEPTH) ** -0.5
    return {
        'x': nrm((BATCH, SEQ, D_MODEL), 1.0),
        'attn_norm': gain((L, D_MODEL)),
        'w_in': nrm((L, D_MODEL, N_IN), D_MODEL ** -0.5),
        'nsa_cmp_pos': nrm((L, 2, CMP_LEN, HEAD_DIM), 0.1),
        'nsa_cmp_w1': nrm((L, 2, CMP_LEN * HEAD_DIM, CMP_HIDDEN), (CMP_LEN * HEAD_DIM) ** -0.5),
        'nsa_cmp_w2': nrm((L, 2, CMP_HIDDEN, HEAD_DIM), CMP_HIDDEN ** -0.5),
        'nsa_out_gain': gain((L, D_NSA)),
        'rw_mu': jax.random.uniform(next(ks), (L, N_IN_RWKV), jnp.float32),
        'rw_w0': nrm((L, D_RWKV), 0.5),
        'rw_w2': nrm((L, RW_DECAY_LORA, D_RWKV), 0.5 * RW_DECAY_LORA ** -0.5),
        'rw_a0': nrm((L, D_RWKV), 0.5),
        'rw_a2': nrm((L, RW_AAA_LORA, D_RWKV), 0.5 * RW_AAA_LORA ** -0.5),
        'rw_g2': nrm((L, RW_GATE_LORA, D_RWKV), RW_GATE_LORA ** -0.5),
        'rw_k_k': 1.0 + nrm((L, D_RWKV), 0.1),
        'rw_k_a': 1.0 + nrm((L, D_RWKV), 0.1),
        'rw_r_k': nrm((L, RWKV_HEADS, HEAD_DIM), 0.1),
        'rw_lnx_w': gain((L, D_RWKV)),
        'rw_lnx_b': nrm((L, D_RWKV), 0.01),
        'moba_out_gain': gain((L, D_MOBA)),
        'w_out': nrm((L, D_MIX, D_MODEL), D_MIX ** -0.5 * resid),
        'ffn_norm': gain((L, D_MODEL)),
        'ffn_w_in': nrm((L, D_MODEL, 2 * D_FF), D_MODEL ** -0.5),
        'ffn_conv_w': nrm((L, CONV_WIDTH, D_FF), CONV_WIDTH ** -0.5),
        'ffn_conv_b': nrm((L, D_FF), 0.01),
        'ffn_w_out': nrm((L, D_FF, D_MODEL), D_FF ** -0.5 * resid),
        'final_norm': gain((D_MODEL,)),
    }


def reference(x, attn_norm, w_in, nsa_cmp_pos, nsa_cmp_w1, nsa_cmp_w2, nsa_out_gain, rw_mu, rw_w0, rw_w2, rw_a0, rw_a2, rw_g2, rw_k_k, rw_k_a, rw_r_k, rw_lnx_w, rw_lnx_b, moba_out_gain, w_out, ffn_norm, ffn_w_in, ffn_conv_w, ffn_conv_b, ffn_w_out, final_norm):
    for l in range(DEPTH):
        h = _rmsnorm(x, attn_norm[l])
        proj = h @ w_in[l]
        p_nsa, p_rw, p_moba = _split(proj, [N_IN_NSA, N_IN_RWKV, N_IN_MOBA])
        o_nsa = _head_rmsnorm(_nsa_mixer(p_nsa, nsa_cmp_pos[l], nsa_cmp_w1[l], nsa_cmp_w2[l]), nsa_out_gain[l])
        o_rw = _rwkv7_mixer(p_rw, rw_mu[l], rw_w0[l], rw_w2[l], rw_a0[l], rw_a2[l], rw_g2[l], rw_k_k[l], rw_k_a[l], rw_r_k[l], rw_lnx_w[l], rw_lnx_b[l])
        o_moba = _head_rmsnorm(_moba_mixer(p_moba), moba_out_gain[l])
        x = x + jnp.concatenate([o_nsa, o_rw, o_moba], axis=-1) @ w_out[l]
        x = x + _conv_glu(_rmsnorm(x, ffn_norm[l]), ffn_w_in[l], ffn_conv_w[l], ffn_conv_b[l], ffn_w_out[l])
    return _rmsnorm(x, final_norm)
```

```python
import functools

import numpy as np
import jax
import jax.numpy as jnp
from jax import lax
from jax.experimental import pallas as pl
from jax.experimental.pallas import tpu as pltpu

HEAD_DIM = 64
NSA_HEADS = 6
NSA_KV_HEADS = 2
NSA_GROUP = NSA_HEADS // NSA_KV_HEADS
RWKV_HEADS = 4
MOBA_HEADS = 6
D_NSA = NSA_HEADS * HEAD_DIM
D_NSA_KV = NSA_KV_HEADS * HEAD_DIM
D_RWKV = RWKV_HEADS * HEAD_DIM
D_MOBA = MOBA_HEADS * HEAD_DIM
CMP_LEN = 32
CMP_STRIDE = 16
SLC_BLOCK = 64
SLC_TOPK = 16
WINDOW = 512
N_BRANCH = 3
RW_DECAY_SCALE = 0.606531
RW_LN_EPS = 64e-5
MOBA_BLOCK = 256
MOBA_TOPK = 3
CONV_WIDTH = 3
NORM_EPS = 1e-6
BIG = 1e9

N_GATE = NSA_HEADS * N_BRANCH
N_IN_NSA = D_NSA + 6 * D_NSA_KV + N_GATE
N_IN_RWKV = 3 * D_RWKV + 128
N_IN_MOBA = 3 * D_MOBA
GATE_PAD = 128 - N_GATE
OFF_RW = N_IN_NSA + GATE_PAD
OFF_MOBA = OFF_RW + N_IN_RWKV
N_PROJ = OFF_MOBA + N_IN_MOBA

NEG = -1e30
SCALE = HEAD_DIM ** -0.5
NSA_TQ = 128
NSA_TK = 256
RW_CHUNK = 64
VMEM_LIMIT = 56 * 1024 * 1024

F32 = jnp.float32
BF16 = jnp.bfloat16
HI = lax.Precision.HIGHEST


def _params(n_axes):
    return pltpu.CompilerParams(dimension_semantics=("arbitrary",) * n_axes, vmem_limit_bytes=VMEM_LIMIT)


def _mm(a, b, precision=None):
    return jnp.dot(a, b, preferred_element_type=F32, precision=precision)


def _rms(x, gain):
    return x * lax.rsqrt(jnp.mean(x * x, axis=-1, keepdims=True) + NORM_EPS) * gain


def _norm_proj_body(x_ref, g_ref, w_ref, o_ref):
    h = _rms(x_ref[...], g_ref[...]).astype(BF16)
    o_ref[...] = _mm(h, w_ref[...])


def _norm_proj(x2d, gain, w, tm=512):
    m, d = x2d.shape
    n = w.shape[1]
    return pl.pallas_call(
        _norm_proj_body,
        grid=(m // tm,),
        in_specs=[pl.BlockSpec((tm, d), lambda i: (i, 0)), pl.BlockSpec((1, d), lambda i: (0, 0)),
                  pl.BlockSpec((d, n), lambda i: (0, 0))],
        out_specs=pl.BlockSpec((tm, n), lambda i: (i, 0)),
        out_shape=jax.ShapeDtypeStruct((m, n), F32),
        compiler_params=_params(1), name="norm_proj")(x2d, gain.reshape(1, d), w)


def _out_proj_body(x_ref, a_ref, b_ref, c_ref, wa_ref, wb_ref, wc_ref, o_ref):
    o_ref[...] = (x_ref[...] + _mm(a_ref[...], wa_ref[...]) + _mm(b_ref[...], wb_ref[...])
                  + _mm(c_ref[...], wc_ref[...]))


def _out_proj(x2d, a, b, c, wa, wb, wc, tm=512):
    m, d = x2d.shape
    row = lambda w: pl.BlockSpec((tm, w), lambda i: (i, 0))
    full = lambda arr: pl.BlockSpec(arr.shape, lambda i: (0, 0))
    return pl.pallas_call(
        _out_proj_body,
        grid=(m // tm,),
        in_specs=[row(d), row(a.shape[1]), row(b.shape[1]), row(c.shape[1]), full(wa), full(wb), full(wc)],
        out_specs=row(d),
        out_shape=jax.ShapeDtypeStruct((m, d), F32),
        compiler_params=_params(1), name="out_proj")(x2d, a, b, c, wa, wb, wc)


def _ffn_body(x_ref, g_ref, wu_ref, wg_ref, cw_ref, cb_ref, wo_ref, fg_ref, o_ref, carry_ref, *, tf, final_norm):
    @pl.when(pl.program_id(1) == 0)
    def _():
        carry_ref[...] = jnp.zeros_like(carry_ref)

    x = x_ref[0]
    tm = x.shape[0]
    h = _rms(x, g_ref[...]).astype(BF16)
    row = lax.broadcasted_iota(jnp.int32, (tm, tf), 0)
    acc = x
    for c in range(wu_ref.shape[1] // tf):
        cs = slice(c * tf, (c + 1) * tf)
        u = _mm(h, wu_ref[:, cs])
        g = _mm(h, wg_ref[:, cs])
        prev = carry_ref[:, cs]
        g1 = jnp.where(row == 0, prev[7:8, :], pltpu.roll(g, 1, axis=0))
        g2 = jnp.where(row == 0, prev[6:7, :], jnp.where(row == 1, prev[7:8, :], pltpu.roll(g, 2, axis=0)))
        carry_ref[:, cs] = g[tm - 8:, :]
        cw = cw_ref[:, cs]
        gc = cw[0:1, :] * g2 + cw[1:2, :] * g1 + cw[2:3, :] * g + cb_ref[:, cs]
        act = (gc * jax.nn.sigmoid(gc) * u).astype(BF16)
        acc = acc + _mm(act, wo_ref[cs, :])
    if final_norm:
        acc = _rms(acc, fg_ref[...])
    o_ref[0] = acc


def _ffn(x, gain, wu, wg, cw, cb, wo, fgain, final_norm, tm=256, tf=1408):
    b, s, d = x.shape
    dff = wu.shape[1]
    full = lambda arr: pl.BlockSpec(arr.shape, lambda i, j: (0, 0))
    gain = gain.reshape(1, d)
    cb = cb.reshape(1, dff)
    fgain = fgain.reshape(1, d)
    return pl.pallas_call(
        functools.partial(_ffn_body, tf=tf, final_norm=final_norm),
        grid=(b, s // tm),
        in_specs=[pl.BlockSpec((1, tm, d), lambda i, j: (i, j, 0)), full(gain), full(wu), full(wg), full(cw),
                  full(cb), full(wo), full(fgain)],
        out_specs=pl.BlockSpec((1, tm, d), lambda i, j: (i, j, 0)),
        out_shape=jax.ShapeDtypeStruct((b, s, d), F32),
        scratch_shapes=[pltpu.VMEM((8, dff), F32)],
        compiler_params=_params(2), name="conv_glu")(x, gain, wu, wg, cw, cb, wo, fgain)


def _attend(carry, s, vT):
    m, l, acc = carry
    m_new = jnp.maximum(m, jnp.max(s, axis=0, keepdims=True))
    alpha = jnp.exp(m - m_new)
    p = jnp.exp(s - m_new)
    l = alpha * l + jnp.sum(p, axis=0, keepdims=True)
    acc = alpha * acc + _mm(vT, p.astype(BF16))
    return m_new, l, acc


def _attend_init(nq):
    return (jnp.full((1, nq), NEG, F32), jnp.zeros((1, nq), F32), jnp.zeros((HEAD_DIM, nq), F32))


def _moba_body(q_ref, k_ref, vT_ref, gain_ref, o_ref, kmean_ref, bias_ref, *, nb):
    qi = pl.program_id(2)
    blk = MOBA_BLOCK

    @pl.when(qi == 0)
    def _():
        kmean_ref[...] = jnp.mean(k_ref[0, 0].astype(F32), axis=1)

    q = q_ref[0, 0]
    gate = _mm(kmean_ref[...], q.astype(F32), HI)
    bid = lax.broadcasted_iota(jnp.int32, (nb, blk), 0)
    gate = jnp.where(bid < qi, gate, -jnp.inf)
    sel = jnp.zeros((nb, blk), F32)
    for _ in range(min(MOBA_TOPK, max(nb - 1, 1))):
        m = jnp.max(gate, axis=0, keepdims=True)
        cand = (gate == m) & (m > -jnp.inf)
        idx = jnp.min(jnp.where(cand, bid, nb), axis=0, keepdims=True)
        hit = bid == idx
        sel = jnp.where(hit, 1.0, sel)
        gate = jnp.where(hit, -jnp.inf, gate)
    bias_ref[...] = jnp.where(sel > 0.5, 0.0, NEG)

    s = _mm(k_ref[0, 0, qi], q) * SCALE
    kpos = lax.broadcasted_iota(jnp.int32, (blk, blk), 0)
    qpos = lax.broadcasted_iota(jnp.int32, (blk, blk), 1)
    carry = _attend(_attend_init(blk), jnp.where(kpos <= qpos, s, NEG), vT_ref[0, 0, qi])

    def body(j, carry):
        s = _mm(k_ref[0, 0, j], q) * SCALE + bias_ref[pl.ds(j, 1), :]
        return _attend(carry, s, vT_ref[0, 0, j])

    _, l, acc = lax.fori_loop(0, qi, body, carry)
    o = acc / jnp.maximum(l, 1e-30)
    o = o * lax.rsqrt(jnp.mean(o * o, axis=0, keepdims=True) + NORM_EPS) * gain_ref[0]
    o_ref[0, 0] = o.astype(o_ref.dtype)


def _moba(q, k, v, gain):
    b, s, _ = q.shape
    h, blk = MOBA_HEADS, MOBA_BLOCK
    nb = s // blk
    heads = lambda t: t.astype(BF16).reshape(b, nb, blk, h, HEAD_DIM)
    qT = jnp.transpose(heads(q), (0, 3, 4, 1, 2)).reshape(b, h, HEAD_DIM, s)
    kb = jnp.transpose(heads(k), (0, 3, 1, 2, 4))
    vT = jnp.transpose(heads(v), (0, 3, 1, 4, 2))
    oT = pl.pallas_call(
        functools.partial(_moba_body, nb=nb),
        grid=(b, h, nb),
        in_specs=[pl.BlockSpec((1, 1, HEAD_DIM, blk), lambda i, j, c: (i, j, 0, c)),
                  pl.BlockSpec((1, 1, nb, blk, HEAD_DIM), lambda i, j, c: (i, j, 0, 0, 0)),
                  pl.BlockSpec((1, 1, nb, HEAD_DIM, blk), lambda i, j, c: (i, j, 0, 0, 0)),
                  pl.BlockSpec((1, HEAD_DIM, 1), lambda i, j, c: (j, 0, 0))],
        out_specs=pl.BlockSpec((1, 1, HEAD_DIM, blk), lambda i, j, c: (i, j, 0, c)),
        out_shape=jax.ShapeDtypeStruct((b, h, HEAD_DIM, s), BF16),
        scratch_shapes=[pltpu.VMEM((nb, HEAD_DIM), F32), pltpu.VMEM((nb, blk), F32)],
        compiler_params=_params(3), name="moba")(qT, kb, vT, gain.reshape(h, HEAD_DIM, 1))
    return jnp.transpose(oT, (0, 3, 1, 2)).reshape(b, s, h * HEAD_DIM)


def _gelu_tanh(x):
    return x * (0.5 * (1.0 + jnp.tanh(np.sqrt(2.0 / np.pi) * (x + 0.044715 * (x * x * x)))))


def _nsa_cmp_body(r_ref, pos_ref, wtop_ref, wbot_ref, w2_ref, o_ref):
    r = r_ref[0, 0]
    nc = r.shape[0]
    y = _mm(r + pos_ref[0, 0], wtop_ref[0], HI)
    z = _mm(r + pos_ref[0, 1], wbot_ref[0], HI)
    pre = y + pltpu.roll(z, nc - 1, axis=0)
    o_ref[0, 0] = _mm(_gelu_tanh(pre), w2_ref[0], HI)


def _nsa_compress(kv, pos, w1, w2):
    b, _, s, _ = kv.shape
    g, d = NSA_KV_HEADS, HEAD_DIM
    nc = s // CMP_STRIDE
    hid = w1.shape[-1]
    half = CMP_LEN // 2
    eye = jnp.eye(g, dtype=F32)
    w1r = w1.reshape(2, 2, half, d, hid)
    w1p = jnp.einsum('thjdc,gk->thjgdkc', w1r, eye).reshape(2, 2, half * g * d, g * hid)
    posp = jnp.broadcast_to(pos.reshape(2, 2, half, 1, d), (2, 2, half, g, d)).reshape(2, 2, 1, half * g * d)
    w2p = jnp.einsum('tcd,gk->tgckd', w2, eye).reshape(2, g * hid, g * d)
    r = kv.reshape(b, 2, nc, half * g * d)
    kd = half * g * d
    return pl.pallas_call(
        _nsa_cmp_body,
        grid=(2, b),
        in_specs=[pl.BlockSpec((1, 1, nc, kd), lambda t, i: (i, t, 0, 0)),
                  pl.BlockSpec((1, 2, 1, kd), lambda t, i: (t, 0, 0, 0)),
                  pl.BlockSpec((1, kd, g * hid), lambda t, i: (t, 0, 0)),
                  pl.BlockSpec((1, kd, g * hid), lambda t, i: (t, 0, 0)),
                  pl.BlockSpec((1, g * hid, g * d), lambda t, i: (t, 0, 0))],
        out_specs=pl.BlockSpec((1, 1, nc, g * d), lambda t, i: (i, t, 0, 0)),
        out_shape=jax.ShapeDtypeStruct((b, 2, nc, g * d), F32),
        compiler_params=_params(2), name="nsa_compress")(r, posp, w1p[:, 0], w1p[:, 1], w2p)


def _cmp_to_slc_T(nc, n_slc):
    r = SLC_BLOCK // CMP_STRIDE
    c = CMP_LEN // CMP_STRIDE
    i = (r * np.arange(n_slc)[:, None, None] - np.arange(r)[None, :, None] - np.arange(c)[None, None, :]).reshape(n_slc, -1)
    m = (i[:, :, None] == np.arange(nc - 1)[None, None, :]).sum(1)
    return np.concatenate([m, np.zeros((n_slc, 1), m.dtype)], axis=1).astype(np.float32)


def _nsa_attn_body(q_ref, gt_ref, kc_ref, vcT_ref, map_ref, ks_ref, vsT_ref, kw_ref, vwT_ref, gain_ref, o_ref,
                   bias_ref, *, nc, n_slc):
    c = pl.program_id(2)
    tq, rr = NSA_TQ, NSA_GROUP
    nl = tq * rr
    t0 = c * tq
    q = q_ref[0, 0, 0]
    lane = lax.broadcasted_iota(jnp.int32, (1, nl), 1)
    tpos3 = t0 + (lane & (tq - 1))

    sc = _mm(kc_ref[0, 0], q.astype(F32), HI) * SCALE
    n_id = lax.broadcasted_iota(jnp.int32, (nc, nl), 0)
    cmask = (n_id * CMP_STRIDE + (CMP_LEN - 1)) <= tpos3
    scm = jnp.where(cmask, sc, NEG)
    pc = jnp.where(cmask, jnp.exp(scm - jnp.max(scm, axis=0, keepdims=True)), 0.0)
    pc = pc / jnp.maximum(jnp.sum(pc, axis=0, keepdims=True), 1e-30)
    oc = _mm(vcT_ref[0, 0], pc.astype(BF16))

    pcs = pc[:, 0:tq]
    for r in range(1, rr):
        pcs = pcs + pc[:, r * tq:(r + 1) * tq]
    imp = _mm(map_ref[...], pcs, HI)
    bid = lax.broadcasted_iota(jnp.int32, (n_slc, tq), 0)
    tpos = t0 + lax.broadcasted_iota(jnp.int32, (1, tq), 1)
    cur = tpos // SLC_BLOCK
    forced = (bid == 0) | (bid == cur) | (bid == cur - 1)
    val = jnp.where(forced, BIG, jnp.where(bid * SLC_BLOCK <= tpos, imp, -BIG))
    sel = jnp.zeros((n_slc, tq), F32)
    for _ in range(min(SLC_TOPK, n_slc)):
        m = jnp.max(val, axis=0, keepdims=True)
        idx = jnp.min(jnp.where(val == m, bid, n_slc), axis=0, keepdims=True)
        hit = bid == idx
        sel = jnp.where(hit, 1.0, sel)
        val = jnp.where(hit, -jnp.inf, val)
    bias_ref[...] = jnp.where(sel > 0.5, 0.0, NEG)

    per_tile = NSA_TK // SLC_BLOCK

    def sel_step(j, carry, diagonal):
        s = _mm(ks_ref[0, 0, j], q) * SCALE
        parts = []
        for i in range(per_tile):
            brow = bias_ref[pl.ds(per_tile * j + i, 1), :]
            parts.append(s[i * SLC_BLOCK:(i + 1) * SLC_BLOCK, :] + jnp.concatenate([brow] * rr, axis=1))
        s = jnp.concatenate(parts, axis=0)
        if diagonal:
            kpos = j * NSA_TK + lax.broadcasted_iota(jnp.int32, (NSA_TK, nl), 0)
            s = jnp.where(kpos <= tpos3, s, NEG)
        return _attend(carry, s, vsT_ref[0, 0, j])

    jl = t0 // NSA_TK
    carry = lax.fori_loop(0, jl, lambda j, cy: sel_step(j, cy, False), _attend_init(nl))
    _, l_s, acc_s = sel_step(jl, carry, True)

    carry = _attend_init(nl)
    n_w = WINDOW // tq
    for i in range(n_w, -1, -1):
        widx = c - n_w + i
        wcl = jnp.maximum(widx, 0)
        s = _mm(kw_ref[0, 0, wcl], q) * SCALE
        kpos = widx * tq + lax.broadcasted_iota(jnp.int32, (tq, nl), 0)
        dist = tpos3 - kpos
        s = jnp.where((dist >= 0) & (dist < WINDOW) & (kpos >= 0), s, NEG)
        carry = _attend(carry, s, vwT_ref[0, 0, wcl])
    _, l_w, acc_w = carry

    g = jax.nn.sigmoid(gt_ref[0, 0, 0])
    o_s = acc_s / jnp.maximum(l_s, 1e-30)
    o_w = acc_w / jnp.maximum(l_w, 1e-30)
    outs = []
    for r in range(rr):
        sl = slice(r * tq, (r + 1) * tq)
        o = g[r:r + 1, :] * oc[:, sl] + g[rr + r:rr + r + 1, :] * o_s[:, sl] + g[2 * rr + r:2 * rr + r + 1, :] * o_w[:, sl]
        o = o * lax.rsqrt(jnp.mean(o * o, axis=0, keepdims=True) + NORM_EPS) * gain_ref[0, r]
        outs.append(o)
    o_ref[0, 0, 0] = jnp.concatenate(outs, axis=1).astype(o_ref.dtype)


def _nsa(p, cmp_pos, cmp_w1, cmp_w2, gain):
    b, s, _ = p.shape
    g, rr, d, tq = NSA_KV_HEADS, NSA_GROUP, HEAD_DIM, NSA_TQ
    nq, nl = s // tq, NSA_TQ * NSA_GROUP
    nc, n_slc = s // CMP_STRIDE, s // SLC_BLOCK
    kvw = g * d
    q = p[..., :D_NSA]
    kv = [p[..., D_NSA + i * kvw:D_NSA + (i + 1) * kvw] for i in range(6)]
    gt = p[..., D_NSA + 6 * kvw:D_NSA + 6 * kvw + N_GATE]

    cmp = _nsa_compress(jnp.stack(kv[0:2], axis=1), cmp_pos, cmp_w1, cmp_w2)
    cmp = cmp.reshape(b, 2, nc, g, d)
    kc = jnp.transpose(cmp[:, 0], (0, 2, 1, 3))
    vcT = jnp.transpose(cmp[:, 1], (0, 2, 3, 1)).astype(BF16)

    qT = jnp.transpose(q.astype(BF16).reshape(b, nq, tq, g, rr, d), (0, 3, 1, 5, 4, 2)).reshape(b, g, nq, d, nl)
    gtT = jnp.transpose(gt.reshape(b, nq, tq, g, rr, N_BRANCH), (0, 3, 1, 5, 4, 2)).reshape(b, g, nq, N_BRANCH * rr, tq)

    def key_tiles(t, tk):
        return jnp.transpose(t.astype(BF16).reshape(b, s // tk, tk, g, d), (0, 3, 1, 2, 4))

    def val_tiles(t, tk):
        return jnp.transpose(t.astype(BF16).reshape(b, s // tk, tk, g, d), (0, 3, 1, 4, 2))

    ks, vsT = key_tiles(kv[2], NSA_TK), val_tiles(kv[3], NSA_TK)
    kw, vwT = key_tiles(kv[4], tq), val_tiles(kv[5], tq)
    cmap = jnp.asarray(_cmp_to_slc_T(nc, n_slc))

    per_bg = lambda shape: pl.BlockSpec((1, 1) + shape, lambda i, j, c: (i, j) + (0,) * len(shape))
    per_tile = lambda shape: pl.BlockSpec((1, 1, 1) + shape, lambda i, j, c: (i, j, c) + (0,) * len(shape))
    oT = pl.pallas_call(
        functools.partial(_nsa_attn_body, nc=nc, n_slc=n_slc),
        grid=(b, g, nq),
        in_specs=[per_tile((d, nl)), per_tile((N_BRANCH * rr, tq)), per_bg((nc, d)), per_bg((d, nc)),
                  pl.BlockSpec((n_slc, nc), lambda i, j, c: (0, 0)),
                  per_bg((s // NSA_TK, NSA_TK, d)), per_bg((s // NSA_TK, d, NSA_TK)),
                  per_bg((nq, tq, d)), per_bg((nq, d, tq)),
                  pl.BlockSpec((1, rr, d, 1), lambda i, j, c: (j, 0, 0, 0))],
        out_specs=per_tile((d, nl)),
        out_shape=jax.ShapeDtypeStruct((b, g, nq, d, nl), BF16),
        scratch_shapes=[pltpu.VMEM((n_slc, tq), F32)],
        compiler_params=_params(3), name="nsa_attn")(qT, gtT, kc, vcT, cmap, ks, vsT, kw, vwT,
                                                      gain.reshape(g, rr, d, 1))
    o = jnp.transpose(oT.reshape(b, g, nq, d, rr, tq), (0, 2, 5, 1, 4, 3))
    return o.reshape(b, s, D_NSA)


def _head_ones():
    i = np.arange(D_RWKV) // HEAD_DIM
    return jnp.asarray((i[:, None] == i[None, :]).astype(np.float32))


def _rw_prep_body(p_ref, mu_ref, w2_ref, a2_ref, g2_ref, vec_ref, ones_ref,
                  r_ref, lw_ref, k_ref, v_ref, kk_ref, b_ref, g_ref, bonus_ref, carry_ref):
    @pl.when(pl.program_id(1) == 0)
    def _():
        carry_ref[...] = jnp.zeros_like(carry_ref)

    p = p_ref[0]
    t = p.shape[0]
    row = lax.broadcasted_iota(jnp.int32, p.shape, 0)
    prev = jnp.where(row == 0, carry_ref[7:8, :], pltpu.roll(p, 1, axis=0))
    carry_ref[...] = p[t - 8:, :]
    xs = p + (prev - p) * mu_ref[...]
    dr = D_RWKV
    r, k, v, lora = xs[:, :dr], xs[:, dr:2 * dr], xs[:, 2 * dr:3 * dr], xs[:, 3 * dr:]
    w0, a0, k_k, k_a, r_k = (vec_ref[i:i + 1, :] for i in range(5))
    ones = ones_ref[...]
    logw = -RW_DECAY_SCALE * jax.nn.sigmoid(w0 + _mm(jnp.tanh(lora), w2_ref[...], HI))
    a = jax.nn.sigmoid(a0 + _mm(lora, a2_ref[...], HI))
    g_ref[0] = _mm(jax.nn.sigmoid(lora), g2_ref[...], HI)
    kk = k * k_k
    kk = kk / jnp.maximum(jnp.sqrt(_mm(kk * kk, ones, HI)), 1e-12)
    k = k * (1.0 + (a - 1.0) * k_a)
    r_ref[0] = r
    lw_ref[0] = logw
    k_ref[0] = k
    v_ref[0] = v
    kk_ref[0] = kk
    b_ref[0] = kk * a
    bonus_ref[0] = _mm(r * k * r_k, ones, HI) * v


def _rw_prep(p, mu, w0, w2, a0, a2, g2, k_k, k_a, r_k, tm=512):
    b, s, n = p.shape
    dr = D_RWKV
    nl = n - 3 * dr
    pad = lambda w, lo: jnp.zeros((nl, dr), F32).at[lo:lo + w.shape[0]].set(w)
    w2p, a2p, g2p = pad(w2, 0), pad(a2, w2.shape[0]), pad(g2, w2.shape[0] + a2.shape[0])
    vec = jnp.concatenate([jnp.stack([w0, a0, k_k, k_a, r_k.reshape(dr)]), jnp.zeros((3, dr), F32)])
    full = lambda arr: pl.BlockSpec(arr.shape, lambda i, j: (0, 0))
    ones = _head_ones()
    mu = mu.reshape(1, n)
    tile = pl.BlockSpec((1, tm, dr), lambda i, j: (i, j, 0))
    return pl.pallas_call(
        _rw_prep_body,
        grid=(b, s // tm),
        in_specs=[pl.BlockSpec((1, tm, n), lambda i, j: (i, j, 0)), full(mu), full(w2p), full(a2p), full(g2p),
                  full(vec), full(ones)],
        out_specs=[tile] * 8,
        out_shape=[jax.ShapeDtypeStruct((b, s, dr), F32)] * 8,
        scratch_shapes=[pltpu.VMEM((8, n), F32)],
        compiler_params=_params(2), name="rwkv_prep")(p, mu, w2p, a2p, g2p, vec, ones)


def _cumsum_rows(x):
    n = x.shape[0]
    row = lax.broadcasted_iota(jnp.int32, x.shape, 0)
    d = 1
    while d < n:
        x = x + jnp.where(row >= d, pltpu.roll(x, d, axis=0), 0.0)
        d *= 2
    return x


def _unit_lower_inverse(n_mat, row, col):
    mm = lambda a, b: _mm(a, b, HI)
    eye = (row == col).astype(F32)
    n8 = jnp.where((row >> 3) == (col >> 3), n_mat, 0.0)
    n8_2 = mm(n8, n8)
    n8_4 = mm(n8_2, n8_2)
    p1 = eye + n8 + n8_2 + mm(n8, n8_2)
    t = p1 + mm(p1, n8_4)
    sh = 4
    while (1 << (sh - 1)) < n_mat.shape[0]:
        off = ((row >> sh) == (col >> sh)) & ((row >> (sh - 1)) != (col >> (sh - 1)))
        t = t + mm(mm(t, jnp.where(off, n_mat, 0.0)), t)
        sh += 1
    return t


def _rw_scan_body(r_ref, lw_ref, k_ref, v_ref, kk_ref, b_ref, o_ref, h_ref, *, n_chunks):
    @pl.when(pl.program_id(2) == 0)
    def _():
        h_ref[...] = jnp.zeros_like(h_ref)

    cs = RW_CHUNK
    mm = lambda a, b: _mm(a, b, HI)
    tn = lambda a, b: lax.dot_general(a, b, (((0,), (0,)), ((), ())), precision=HI, preferred_element_type=F32)
    nt = lambda a, b: lax.dot_general(a, b, (((1,), (1,)), ((), ())), precision=HI, preferred_element_type=F32)
    row = lax.broadcasted_iota(jnp.int32, (cs, cs), 0)
    col = lax.broadcasted_iota(jnp.int32, (cs, cs), 1)
    eye = (row == col).astype(F32)
    h = h_ref[...]
    for c in range(n_chunks):
        sl = slice(c * cs, (c + 1) * cs)
        r, lw, k, v, kk, beta = (ref[0, 0, sl, :] for ref in (r_ref, lw_ref, k_ref, v_ref, kk_ref, b_ref))
        cum = _cumsum_rows(lw)
        tot = cum[cs - 1:cs, :]
        a_t = -kk * jnp.exp(cum - lw)
        r_t = r * jnp.exp(cum)
        b_t = beta * jnp.exp(-cum)
        k_t = k * jnp.exp(-cum)
        b_h = beta * jnp.exp(tot - cum)
        k_h = k * jnp.exp(tot - cum)
        gram = nt(jnp.concatenate([a_t, r_t], axis=0), jnp.concatenate([b_t, k_t], axis=0))
        a_ab = jnp.where(row > col, gram[:cs, :cs], 0.0)
        a_ak = jnp.where(row > col, gram[:cs, cs:], 0.0)
        m_rb = jnp.where(row >= col, gram[cs:, :cs], 0.0)
        m_rk = jnp.where(row >= col, gram[cs:, cs:], 0.0)
        t_inv = _unit_lower_inverse(a_ab, row, col)
        wu = mm(t_inv, jnp.concatenate([a_t, mm(a_ak, v)], axis=1))
        qo = mm(m_rb, wu) + jnp.concatenate([r_t, mm(m_rk, v)], axis=1)
        pd = tn(b_h, wu) + jnp.concatenate([eye * jnp.exp(tot), tn(k_h, v)], axis=1)
        o_ref[0, 0, sl, :] = mm(qo[:, :cs], h) + qo[:, cs:]
        h = mm(pd[:, :cs], h) + pd[:, cs:]
    h_ref[...] = h


def _rw_post_body(o_ref, bonus_ref, g_ref, vec_ref, ones_ref, out_ref):
    o = o_ref[0]
    avg = ones_ref[...] * (1.0 / HEAD_DIM)
    mean = _mm(o, avg, HI)
    ctr = o - mean
    var = _mm(ctr * ctr, avg, HI)
    y = ctr * lax.rsqrt(var + RW_LN_EPS) * vec_ref[0:1, :] + vec_ref[1:2, :]
    out_ref[0] = ((y + bonus_ref[0]) * g_ref[0]).astype(out_ref.dtype)


def _rwkv(p, mu, w0, w2, a0, a2, g2, k_k, k_a, r_k, lnx_w, lnx_b, tb=256, tm=512):
    b, s, _ = p.shape
    hh, d, dr = RWKV_HEADS, HEAD_DIM, D_RWKV
    r, lw, k, v, kk, beta, g, bonus = _rw_prep(p, mu, w0, w2, a0, a2, g2, k_k, k_a, r_k)
    heads = lambda t: jnp.transpose(t.reshape(b, s, hh, d), (0, 2, 1, 3))
    tile = pl.BlockSpec((1, 1, tb, d), lambda i, j, c: (i, j, c, 0))
    o = pl.pallas_call(
        functools.partial(_rw_scan_body, n_chunks=tb // RW_CHUNK),
        grid=(b, hh, s // tb),
        in_specs=[tile] * 6,
        out_specs=tile,
        out_shape=jax.ShapeDtypeStruct((b, hh, s, d), F32),
        scratch_shapes=[pltpu.VMEM((d, d), F32)],
        compiler_params=_params(3), name="rwkv_scan")(*(heads(t) for t in (r, lw, k, v, kk, beta)))
    o = jnp.transpose(o, (0, 2, 1, 3)).reshape(b, s, dr)
    vec = jnp.concatenate([jnp.stack([lnx_w, lnx_b]), jnp.zeros((6, dr), F32)])
    ones = _head_ones()
    tile2 = pl.BlockSpec((1, tm, dr), lambda i, j: (i, j, 0))
    full = lambda arr: pl.BlockSpec(arr.shape, lambda i, j: (0, 0))
    return pl.pallas_call(
        _rw_post_body,
        grid=(b, s // tm),
        in_specs=[tile2, tile2, tile2, full(vec), full(ones)],
        out_specs=tile2,
        out_shape=jax.ShapeDtypeStruct((b, s, dr), BF16),
        compiler_params=_params(2), name="rwkv_post")(o, bonus, g, vec, ones)


def kernel(x, attn_norm, w_in, nsa_cmp_pos, nsa_cmp_w1, nsa_cmp_w2, nsa_out_gain, rw_mu, rw_w0, rw_w2, rw_a0, rw_a2, rw_g2, rw_k_k, rw_k_a, rw_r_k, rw_lnx_w, rw_lnx_b, moba_out_gain, w_out, ffn_norm, ffn_w_in, ffn_conv_w, ffn_conv_b, ffn_w_out, final_norm):
    b, s, d = x.shape
    depth = w_in.shape[0]
    d_ff = ffn_w_out.shape[1]
    w_in_p = jnp.concatenate([w_in[:, :, :N_IN_NSA], jnp.zeros((depth, d, GATE_PAD), w_in.dtype),
                              w_in[:, :, N_IN_NSA:]], axis=-1).astype(BF16)
    w_out_b = w_out.astype(BF16)
    ffn_w_in_b = ffn_w_in.astype(BF16)
    ffn_w_out_b = ffn_w_out.astype(BF16)
    for l in range(depth):
        proj = _norm_proj(x.reshape(b * s, d), attn_norm[l], w_in_p[l]).reshape(b, s, N_PROJ)
        o_nsa = _nsa(proj[..., :OFF_RW], nsa_cmp_pos[l], nsa_cmp_w1[l], nsa_cmp_w2[l], nsa_out_gain[l])
        o_rw = _rwkv(proj[..., OFF_RW:OFF_MOBA], rw_mu[l], rw_w0[l], rw_w2[l], rw_a0[l], rw_a2[l], rw_g2[l],
                     rw_k_k[l], rw_k_a[l], rw_r_k[l], rw_lnx_w[l], rw_lnx_b[l])
        pm = proj[..., OFF_MOBA:]
        o_moba = _moba(pm[..., :D_MOBA], pm[..., D_MOBA:2 * D_MOBA], pm[..., 2 * D_MOBA:], moba_out_gain[l])
        x = _out_proj(x.reshape(b * s, d), o_nsa.reshape(b * s, D_NSA), o_rw.reshape(b * s, D_RWKV),
                      o_moba.reshape(b * s, D_MOBA), w_out_b[l, :D_NSA], w_out_b[l, D_NSA:D_NSA + D_RWKV],
                      w_out_b[l, D_NSA + D_RWKV:]).reshape(b, s, d)
        x = _ffn(x, ffn_norm[l], ffn_w_in_b[l, :, :d_ff], ffn_w_in_b[l, :, d_ff:], ffn_conv_w[l], ffn_conv_b[l],
                 ffn_w_out_b[l], final_norm, final_norm=(l == depth - 1))
    return x
```

```python
import functools

import numpy as np
import jax
import jax.numpy as jnp
from jax import lax
from jax.experimental import pallas as pl
from jax.experimental.pallas import tpu as pltpu

HEAD_DIM = 64
NSA_HEADS = 6
NSA_KV_HEADS = 2
NSA_GROUP = NSA_HEADS // NSA_KV_HEADS
RWKV_HEADS = 4
MOBA_HEADS = 6
D_NSA = NSA_HEADS * HEAD_DIM
D_NSA_KV = NSA_KV_HEADS * HEAD_DIM
D_RWKV = RWKV_HEADS * HEAD_DIM
D_MOBA = MOBA_HEADS * HEAD_DIM
CMP_LEN = 32
CMP_STRIDE = 16
SLC_BLOCK = 64
SLC_TOPK = 16
WINDOW = 512
N_BRANCH = 3
RW_DECAY_SCALE = 0.606531
RW_LN_EPS = 64e-5
MOBA_BLOCK = 256
MOBA_TOPK = 3
CONV_WIDTH = 3
NORM_EPS = 1e-6
BIG = 1e9

N_GATE = NSA_HEADS * N_BRANCH
N_IN_NSA = D_NSA + 6 * D_NSA_KV + N_GATE
N_IN_RWKV = 3 * D_RWKV + 128
N_IN_MOBA = 3 * D_MOBA
GATE_PAD = 128 - N_GATE
OFF_RW = N_IN_NSA + GATE_PAD
OFF_MOBA = OFF_RW + N_IN_RWKV
N_PROJ = OFF_MOBA + N_IN_MOBA

NEG = -1e30
SCALE = HEAD_DIM ** -0.5
SCALE_LOG2E = SCALE * float(np.log2(np.e))
NSA_TQ = 128
NSA_TK = 256
RW_CHUNK = 64
VMEM_LIMIT = 56 * 1024 * 1024

F32 = jnp.float32
BF16 = jnp.bfloat16
HI = lax.Precision.HIGHEST


def _params(n_axes):
    return pltpu.CompilerParams(dimension_semantics=("arbitrary",) * n_axes, vmem_limit_bytes=VMEM_LIMIT)


def _mm(a, b, precision=None):
    return jnp.dot(a, b, preferred_element_type=F32, precision=precision)


def _rms(x, gain):
    return x * lax.rsqrt(jnp.mean(x * x, axis=-1, keepdims=True) + NORM_EPS) * gain


def _norm_proj_body(x_ref, g_ref, w_ref, o_ref):
    h = _rms(x_ref[...], g_ref[...]).astype(BF16)
    o_ref[...] = _mm(h, w_ref[...])


def _norm_proj(x2d, gain, w, tm=512):
    m, d = x2d.shape
    n = w.shape[1]
    return pl.pallas_call(
        _norm_proj_body,
        grid=(m // tm,),
        in_specs=[pl.BlockSpec((tm, d), lambda i: (i, 0)), pl.BlockSpec((1, d), lambda i: (0, 0)),
                  pl.BlockSpec((d, n), lambda i: (0, 0))],
        out_specs=pl.BlockSpec((tm, n), lambda i: (i, 0)),
        out_shape=jax.ShapeDtypeStruct((m, n), F32),
        compiler_params=_params(1), name="norm_proj")(x2d, gain.reshape(1, d), w)


def _out_proj_body(x_ref, a_ref, b_ref, c_ref, wa_ref, wb_ref, wc_ref, o_ref):
    o_ref[...] = (x_ref[...] + _mm(a_ref[...], wa_ref[...]) + _mm(b_ref[...], wb_ref[...])
                  + _mm(c_ref[...], wc_ref[...]))


def _out_proj(x2d, a, b, c, wa, wb, wc, tm=512):
    m, d = x2d.shape
    row = lambda w: pl.BlockSpec((tm, w), lambda i: (i, 0))
    full = lambda arr: pl.BlockSpec(arr.shape, lambda i: (0, 0))
    return pl.pallas_call(
        _out_proj_body,
        grid=(m // tm,),
        in_specs=[row(d), row(a.shape[1]), row(b.shape[1]), row(c.shape[1]), full(wa), full(wb), full(wc)],
        out_specs=row(d),
        out_shape=jax.ShapeDtypeStruct((m, d), F32),
        compiler_params=_params(1), name="out_proj")(x2d, a, b, c, wa, wb, wc)


def _ffn_body(x_ref, g_ref, wu_ref, wg_ref, cw_ref, cb_ref, wo_ref, fg_ref, o_ref, carry_ref, *, tf, final_norm):
    @pl.when(pl.program_id(1) == 0)
    def _():
        carry_ref[...] = jnp.zeros_like(carry_ref)

    x = x_ref[0]
    tm = x.shape[0]
    h = _rms(x, g_ref[...]).astype(BF16)
    row = lax.broadcasted_iota(jnp.int32, (tm, tf), 0)
    acc = x
    for c in range(wu_ref.shape[1] // tf):
        cs = slice(c * tf, (c + 1) * tf)
        u = _mm(h, wu_ref[:, cs])
        g = _mm(h, wg_ref[:, cs])
        prev = carry_ref[:, cs]
        g1 = jnp.where(row == 0, prev[7:8, :], pltpu.roll(g, 1, axis=0))
        g2 = jnp.where(row == 0, prev[6:7, :], jnp.where(row == 1, prev[7:8, :], pltpu.roll(g, 2, axis=0)))
        carry_ref[:, cs] = g[tm - 8:, :]
        cw = cw_ref[:, cs]
        gc = cw[0:1, :] * g2 + cw[1:2, :] * g1 + cw[2:3, :] * g + cb_ref[:, cs]
        act = (gc * jax.nn.sigmoid(gc) * u).astype(BF16)
        acc = acc + _mm(act, wo_ref[cs, :])
    if final_norm:
        acc = _rms(acc, fg_ref[...])
    o_ref[0] = acc


def _ffn(x, gain, wu, wg, cw, cb, wo, fgain, final_norm, tm=256, tf=1408):
    b, s, d = x.shape
    dff = wu.shape[1]
    full = lambda arr: pl.BlockSpec(arr.shape, lambda i, j: (0, 0))
    gain = gain.reshape(1, d)
    cb = cb.reshape(1, dff)
    fgain = fgain.reshape(1, d)
    return pl.pallas_call(
        functools.partial(_ffn_body, tf=tf, final_norm=final_norm),
        grid=(b, s // tm),
        in_specs=[pl.BlockSpec((1, tm, d), lambda i, j: (i, j, 0)), full(gain), full(wu), full(wg), full(cw),
                  full(cb), full(wo), full(fgain)],
        out_specs=pl.BlockSpec((1, tm, d), lambda i, j: (i, j, 0)),
        out_shape=jax.ShapeDtypeStruct((b, s, d), F32),
        scratch_shapes=[pltpu.VMEM((8, dff), F32)],
        compiler_params=_params(2), name="conv_glu")(x, gain, wu, wg, cw, cb, wo, fgain)


def _fold8(x, op):
    return op(x.reshape(x.shape[0] // 8, 8, x.shape[1]), axis=0)


UNROLLS = (4, 2, 1)


def _grouped_loop(n, body, init):
    start, carry = 0, init
    for width in UNROLLS:
        count = (n - start) // width

        def group(i, c, width=width, start=start):
            for u in range(width):
                c = body(start + i * width + u, c)
            return c

        carry = lax.fori_loop(0, count, group, carry)
        start = start + count * width
    return carry


def _moba_body(q_ref, k_ref, vT_ref, gain_ref, o_ref, kmean_ref, bias_ref, s_ref, *, nb):
    qi = pl.program_id(2)
    blk = MOBA_BLOCK

    @pl.when(qi == 0)
    def _():
        kmean_ref[...] = jnp.mean(k_ref[0, 0].astype(F32), axis=1)

    q = q_ref[0, 0]
    km_hi, km_lo = _split_bf16(kmean_ref[...])
    gate = _mm(km_hi, q) + _mm(km_lo, q)
    bid = lax.broadcasted_iota(jnp.int32, (nb, blk), 0)
    gate = jnp.where(bid < qi, gate, -jnp.inf)
    sel = jnp.zeros((nb, blk), F32)
    for _ in range(min(MOBA_TOPK, max(nb - 1, 1))):
        m = jnp.max(gate, axis=0, keepdims=True)
        cand = (gate == m) & (m > -jnp.inf)
        idx = jnp.min(jnp.where(cand, bid, nb), axis=0, keepdims=True)
        hit = bid == idx
        sel = jnp.where(hit, 1.0, sel)
        gate = jnp.where(hit, -jnp.inf, gate)
    bias_ref[...] = jnp.where(sel > 0.5, 0.0, NEG)

    s = _mm(k_ref[0, 0, qi], q) * SCALE_LOG2E
    kpos = lax.broadcasted_iota(jnp.int32, (blk, blk), 0)
    qpos = lax.broadcasted_iota(jnp.int32, (blk, blk), 1)
    s = jnp.where(kpos <= qpos, s, NEG)
    s_ref[qi] = s

    def score(j, m8):
        s = _mm(k_ref[0, 0, j], q) * SCALE_LOG2E + bias_ref[pl.ds(j, 1), :]
        s_ref[j] = s
        return jnp.maximum(m8, _fold8(s, jnp.max))

    m = jnp.max(_grouped_loop(qi, score, _fold8(s, jnp.max)), axis=0, keepdims=True)

    def accum(j, carry):
        l8, acc = carry
        p = jnp.exp2(s_ref[j] - m)
        return l8 + _fold8(p, jnp.sum), acc + _mm(vT_ref[0, 0, j], p.astype(BF16))

    l8, acc = _grouped_loop(qi + 1, accum, (jnp.zeros((8, blk), F32), jnp.zeros((HEAD_DIM, blk), F32)))
    o = acc / jnp.maximum(jnp.sum(l8, axis=0, keepdims=True), 1e-30)
    o = o * lax.rsqrt(jnp.mean(o * o, axis=0, keepdims=True) + NORM_EPS) * gain_ref[0]
    o_ref[0, 0] = o.astype(o_ref.dtype)


def _moba(q, k, v, gain):
    b, s, _ = q.shape
    h, blk = MOBA_HEADS, MOBA_BLOCK
    nb = s // blk
    heads = lambda t: t.astype(BF16).reshape(b, nb, blk, h, HEAD_DIM)
    qT = jnp.transpose(heads(q), (0, 3, 4, 1, 2)).reshape(b, h, HEAD_DIM, s)
    kb = jnp.transpose(heads(k), (0, 3, 1, 2, 4))
    vT = jnp.transpose(heads(v), (0, 3, 1, 4, 2))
    oT = pl.pallas_call(
        functools.partial(_moba_body, nb=nb),
        grid=(b, h, nb),
        in_specs=[pl.BlockSpec((1, 1, HEAD_DIM, blk), lambda i, j, c: (i, j, 0, c)),
                  pl.BlockSpec((1, 1, nb, blk, HEAD_DIM), lambda i, j, c: (i, j, 0, 0, 0)),
                  pl.BlockSpec((1, 1, nb, HEAD_DIM, blk), lambda i, j, c: (i, j, 0, 0, 0)),
                  pl.BlockSpec((1, HEAD_DIM, 1), lambda i, j, c: (j, 0, 0))],
        out_specs=pl.BlockSpec((1, 1, HEAD_DIM, blk), lambda i, j, c: (i, j, 0, c)),
        out_shape=jax.ShapeDtypeStruct((b, h, HEAD_DIM, s), BF16),
        scratch_shapes=[pltpu.VMEM((nb, HEAD_DIM), F32), pltpu.VMEM((nb, blk), F32),
                        pltpu.VMEM((nb, blk, blk), F32)],
        compiler_params=_params(3), name="moba")(qT, kb, vT, gain.reshape(h, HEAD_DIM, 1))
    return jnp.transpose(oT, (0, 3, 1, 2)).reshape(b, s, h * HEAD_DIM)


def _gelu_tanh(x):
    return x * (0.5 * (1.0 + jnp.tanh(np.sqrt(2.0 / np.pi) * (x + 0.044715 * (x * x * x)))))


def _nsa_cmp_body(r_ref, pos_ref, wtop_ref, wbot_ref, w2_ref, o_ref):
    r = r_ref[0, 0]
    nc = r.shape[0]
    y = _mm(r + pos_ref[0, 0], wtop_ref[0], HI)
    z = _mm(r + pos_ref[0, 1], wbot_ref[0], HI)
    pre = y + pltpu.roll(z, nc - 1, axis=0)
    o_ref[0, 0] = _mm(_gelu_tanh(pre), w2_ref[0], HI)


def _nsa_compress(kv, pos, w1, w2):
    b, _, s, _ = kv.shape
    g, d = NSA_KV_HEADS, HEAD_DIM
    nc = s // CMP_STRIDE
    hid = w1.shape[-1]
    half = CMP_LEN // 2
    eye = jnp.eye(g, dtype=F32)
    w1r = w1.reshape(2, 2, half, d, hid)
    w1p = jnp.einsum('thjdc,gk->thjgdkc', w1r, eye).reshape(2, 2, half * g * d, g * hid)
    posp = jnp.broadcast_to(pos.reshape(2, 2, half, 1, d), (2, 2, half, g, d)).reshape(2, 2, 1, half * g * d)
    w2p = jnp.einsum('tcd,gk->tgckd', w2, eye).reshape(2, g * hid, g * d)
    r = kv.reshape(b, 2, nc, half * g * d)
    kd = half * g * d
    return pl.pallas_call(
        _nsa_cmp_body,
        grid=(2, b),
        in_specs=[pl.BlockSpec((1, 1, nc, kd), lambda t, i: (i, t, 0, 0)),
                  pl.BlockSpec((1, 2, 1, kd), lambda t, i: (t, 0, 0, 0)),
                  pl.BlockSpec((1, kd, g * hid), lambda t, i: (t, 0, 0)),
                  pl.BlockSpec((1, kd, g * hid), lambda t, i: (t, 0, 0)),
                  pl.BlockSpec((1, g * hid, g * d), lambda t, i: (t, 0, 0))],
        out_specs=pl.BlockSpec((1, 1, nc, g * d), lambda t, i: (i, t, 0, 0)),
        out_shape=jax.ShapeDtypeStruct((b, 2, nc, g * d), F32),
        compiler_params=_params(2), name="nsa_compress")(r, posp, w1p[:, 0], w1p[:, 1], w2p)


def _cmp_to_slc_T(nc, n_slc):
    r = SLC_BLOCK // CMP_STRIDE
    c = CMP_LEN // CMP_STRIDE
    i = (r * np.arange(n_slc)[:, None, None] - np.arange(r)[None, :, None] - np.arange(c)[None, None, :]).reshape(n_slc, -1)
    m = (i[:, :, None] == np.arange(nc - 1)[None, None, :]).sum(1)
    return np.concatenate([m, np.zeros((n_slc, 1), m.dtype)], axis=1).astype(np.float32)


def _nsa_attn_body(q_ref, gt_ref, kc_ref, vcT_ref, map_ref, ks_ref, vsT_ref, kw_ref, vwT_ref, gain_ref, o_ref,
                   bias_ref, s_ref, *, nc, n_slc):
    c = pl.program_id(2)
    tq, rr, d = NSA_TQ, NSA_GROUP, HEAD_DIM
    nl = tq * rr
    t0 = c * tq
    q = q_ref[0, 0, 0]
    lane = lax.broadcasted_iota(jnp.int32, (1, nl), 1)
    tpos3 = t0 + (lane & (tq - 1))

    kc_hi, kc_lo = _split_bf16(kc_ref[0, 0])
    sc = (_mm(kc_hi, q) + _mm(kc_lo, q)) * SCALE
    n_id = lax.broadcasted_iota(jnp.int32, (nc, nl), 0)
    cmask = (n_id * CMP_STRIDE + (CMP_LEN - 1)) <= tpos3
    scm = jnp.where(cmask, sc, NEG)
    pc = jnp.where(cmask, jnp.exp(scm - jnp.max(scm, axis=0, keepdims=True)), 0.0)
    pc = pc / jnp.maximum(jnp.sum(pc, axis=0, keepdims=True), 1e-30)
    oc = _mm(vcT_ref[0, 0], pc.astype(BF16))

    pcs = pc[:, 0:tq]
    for r in range(1, rr):
        pcs = pcs + pc[:, r * tq:(r + 1) * tq]
    p_hi, p_lo = _split_bf16(pcs)
    imp = _mm(map_ref[...], p_hi) + _mm(map_ref[...], p_lo)
    bid = lax.broadcasted_iota(jnp.int32, (n_slc, tq), 0)
    tpos = t0 + lax.broadcasted_iota(jnp.int32, (1, tq), 1)
    cur = tpos // SLC_BLOCK
    forced = (bid == 0) | (bid == cur) | (bid == cur - 1)
    val = jnp.where(forced, BIG, jnp.where(bid * SLC_BLOCK <= tpos, imp, -BIG))
    sel = jnp.zeros((n_slc, tq), F32)
    for _ in range(min(SLC_TOPK, n_slc)):
        m = jnp.max(val, axis=0, keepdims=True)
        idx = jnp.min(jnp.where(val == m, bid, n_slc), axis=0, keepdims=True)
        hit = bid == idx
        sel = jnp.where(hit, 1.0, sel)
        val = jnp.where(hit, -jnp.inf, val)
    bias_ref[...] = jnp.where(sel > 0.5, 0.0, NEG)

    per_tile = NSA_TK // SLC_BLOCK

    def sel_scores(j, diagonal):
        s = _mm(ks_ref[0, 0, j], q) * SCALE_LOG2E
        parts = []
        for i in range(per_tile):
            brow = bias_ref[pl.ds(per_tile * j + i, 1), :]
            parts.append(s[i * SLC_BLOCK:(i + 1) * SLC_BLOCK, :] + jnp.concatenate([brow] * rr, axis=1))
        s = jnp.concatenate(parts, axis=0)
        if diagonal:
            kpos = j * NSA_TK + lax.broadcasted_iota(jnp.int32, (NSA_TK, nl), 0)
            s = jnp.where(kpos <= tpos3, s, NEG)
        s_ref[j] = s
        return _fold8(s, jnp.max)

    jl = t0 // NSA_TK
    m8 = _grouped_loop(jl, lambda j, m8: jnp.maximum(m8, sel_scores(j, False)), sel_scores(jl, True))
    m_s = jnp.max(m8, axis=0, keepdims=True)

    def sel_accum(j, carry):
        l8, acc = carry
        p = jnp.exp2(s_ref[j] - m_s)
        return l8 + _fold8(p, jnp.sum), acc + _mm(vsT_ref[0, 0, j], p.astype(BF16))

    l8, acc_s = _grouped_loop(jl + 1, sel_accum, (jnp.zeros((8, nl), F32), jnp.zeros((d, nl), F32)))
    o_s = acc_s / jnp.maximum(jnp.sum(l8, axis=0, keepdims=True), 1e-30)

    n_w = WINDOW // tq
    w_tiles = []
    for i in range(n_w + 1):
        widx = c - n_w + i
        wcl = jnp.maximum(widx, 0)
        s = _mm(kw_ref[0, 0, wcl], q) * SCALE_LOG2E
        kpos = widx * tq + lax.broadcasted_iota(jnp.int32, (tq, nl), 0)
        dist = tpos3 - kpos
        w_tiles.append((jnp.where((dist >= 0) & (dist < WINDOW) & (kpos >= 0), s, NEG), wcl))
    m8 = functools.reduce(jnp.maximum, [_fold8(s, jnp.max) for s, _ in w_tiles])
    m_w = jnp.max(m8, axis=0, keepdims=True)
    l8, acc_w = jnp.zeros((8, nl), F32), jnp.zeros((d, nl), F32)
    for s, wcl in w_tiles:
        p = jnp.exp2(s - m_w)
        l8, acc_w = l8 + _fold8(p, jnp.sum), acc_w + _mm(vwT_ref[0, 0, wcl], p.astype(BF16))
    o_w = acc_w / jnp.maximum(jnp.sum(l8, axis=0, keepdims=True), 1e-30)

    g = jax.nn.sigmoid(gt_ref[0, 0, 0])
    outs = []
    for r in range(rr):
        sl = slice(r * tq, (r + 1) * tq)
        o = g[r:r + 1, :] * oc[:, sl] + g[rr + r:rr + r + 1, :] * o_s[:, sl] + g[2 * rr + r:2 * rr + r + 1, :] * o_w[:, sl]
        o = o * lax.rsqrt(jnp.mean(o * o, axis=0, keepdims=True) + NORM_EPS) * gain_ref[0, r]
        outs.append(o)
    o_ref[0, 0, 0] = jnp.concatenate(outs, axis=1).astype(o_ref.dtype)


def _nsa(p, cmp_pos, cmp_w1, cmp_w2, gain):
    b, s, _ = p.shape
    g, rr, d, tq = NSA_KV_HEADS, NSA_GROUP, HEAD_DIM, NSA_TQ
    nq, nl = s // tq, NSA_TQ * NSA_GROUP
    nc, n_slc = s // CMP_STRIDE, s // SLC_BLOCK
    kvw = g * d
    q = p[..., :D_NSA]
    kv = [p[..., D_NSA + i * kvw:D_NSA + (i + 1) * kvw] for i in range(6)]
    gt = p[..., D_NSA + 6 * kvw:D_NSA + 6 * kvw + N_GATE]

    cmp = _nsa_compress(jnp.stack(kv[0:2], axis=1), cmp_pos, cmp_w1, cmp_w2)
    cmp = cmp.reshape(b, 2, nc, g, d)
    kc = jnp.transpose(cmp[:, 0], (0, 2, 1, 3))
    vcT = jnp.transpose(cmp[:, 1], (0, 2, 3, 1)).astype(BF16)

    qT = jnp.transpose(q.astype(BF16).reshape(b, nq, tq, g, rr, d), (0, 3, 1, 5, 4, 2)).reshape(b, g, nq, d, nl)
    gtT = jnp.transpose(gt.reshape(b, nq, tq, g, rr, N_BRANCH), (0, 3, 1, 5, 4, 2)).reshape(b, g, nq, N_BRANCH * rr, tq)

    def key_tiles(t, tk):
        return jnp.transpose(t.astype(BF16).reshape(b, s // tk, tk, g, d), (0, 3, 1, 2, 4))

    def val_tiles(t, tk):
        return jnp.transpose(t.astype(BF16).reshape(b, s // tk, tk, g, d), (0, 3, 1, 4, 2))

    ks, vsT = key_tiles(kv[2], NSA_TK), val_tiles(kv[3], NSA_TK)
    kw, vwT = key_tiles(kv[4], tq), val_tiles(kv[5], tq)
    cmap = jnp.asarray(_cmp_to_slc_T(nc, n_slc), BF16)

    per_bg = lambda shape: pl.BlockSpec((1, 1) + shape, lambda i, j, c: (i, j) + (0,) * len(shape))
    per_tile = lambda shape: pl.BlockSpec((1, 1, 1) + shape, lambda i, j, c: (i, j, c) + (0,) * len(shape))
    oT = pl.pallas_call(
        functools.partial(_nsa_attn_body, nc=nc, n_slc=n_slc),
        grid=(b, g, nq),
        in_specs=[per_tile((d, nl)), per_tile((N_BRANCH * rr, tq)), per_bg((nc, d)), per_bg((d, nc)),
                  pl.BlockSpec((n_slc, nc), lambda i, j, c: (0, 0)),
                  per_bg((s // NSA_TK, NSA_TK, d)), per_bg((s // NSA_TK, d, NSA_TK)),
                  per_bg((nq, tq, d)), per_bg((nq, d, tq)),
                  pl.BlockSpec((1, rr, d, 1), lambda i, j, c: (j, 0, 0, 0))],
        out_specs=per_tile((d, nl)),
        out_shape=jax.ShapeDtypeStruct((b, g, nq, d, nl), BF16),
        scratch_shapes=[pltpu.VMEM((n_slc, tq), F32), pltpu.VMEM((s // NSA_TK, NSA_TK, nl), F32)],
        compiler_params=_params(3), name="nsa_attn")(qT, gtT, kc, vcT, cmap, ks, vsT, kw, vwT,
                                                      gain.reshape(g, rr, d, 1))
    o = jnp.transpose(oT.reshape(b, g, nq, d, rr, tq), (0, 2, 5, 1, 4, 3))
    return o.reshape(b, s, D_NSA)


def _head_ones():
    i = np.arange(D_RWKV) // HEAD_DIM
    return jnp.asarray((i[:, None] == i[None, :]).astype(np.float32))


def _rw_prep_body(p_ref, mu_ref, w2_ref, a2_ref, g2_ref, vec_ref, ones_ref,
                  r_ref, lw_ref, k_ref, v_ref, kk_ref, b_ref, g_ref, bonus_ref, carry_ref):
    @pl.when(pl.program_id(1) == 0)
    def _():
        carry_ref[...] = jnp.zeros_like(carry_ref)

    p = p_ref[0]
    t = p.shape[0]
    row = lax.broadcasted_iota(jnp.int32, p.shape, 0)
    prev = jnp.where(row == 0, carry_ref[7:8, :], pltpu.roll(p, 1, axis=0))
    carry_ref[...] = p[t - 8:, :]
    xs = p + (prev - p) * mu_ref[...]
    dr = D_RWKV
    r, k, v, lora = xs[:, :dr], xs[:, dr:2 * dr], xs[:, 2 * dr:3 * dr], xs[:, 3 * dr:]
    w0, a0, k_k, k_a, r_k = (vec_ref[i:i + 1, :] for i in range(5))
    ones = ones_ref[...]
    logw = -RW_DECAY_SCALE * jax.nn.sigmoid(w0 + _mm(jnp.tanh(lora), w2_ref[...], HI))
    a = jax.nn.sigmoid(a0 + _mm(lora, a2_ref[...], HI))
    g_ref[0] = _mm(jax.nn.sigmoid(lora), g2_ref[...], HI)
    kk = k * k_k
    kk = kk / jnp.maximum(jnp.sqrt(_mm(kk * kk, ones, HI)), 1e-12)
    k = k * (1.0 + (a - 1.0) * k_a)
    r_ref[0] = r
    lw_ref[0] = logw
    k_ref[0] = k
    v_ref[0] = v
    kk_ref[0] = kk
    b_ref[0] = kk * a
    bonus_ref[0] = _mm(r * k * r_k, ones, HI) * v


def _rw_prep(p, mu, w0, w2, a0, a2, g2, k_k, k_a, r_k, tm=512):
    b, s, n = p.shape
    dr = D_RWKV
    nl = n - 3 * dr
    pad = lambda w, lo: jnp.zeros((nl, dr), F32).at[lo:lo + w.shape[0]].set(w)
    w2p, a2p, g2p = pad(w2, 0), pad(a2, w2.shape[0]), pad(g2, w2.shape[0] + a2.shape[0])
    vec = jnp.concatenate([jnp.stack([w0, a0, k_k, k_a, r_k.reshape(dr)]), jnp.zeros((3, dr), F32)])
    full = lambda arr: pl.BlockSpec(arr.shape, lambda i, j: (0, 0))
    ones = _head_ones()
    mu = mu.reshape(1, n)
    tile = pl.BlockSpec((1, tm, dr), lambda i, j: (i, j, 0))
    return pl.pallas_call(
        _rw_prep_body,
        grid=(b, s // tm),
        in_specs=[pl.BlockSpec((1, tm, n), lambda i, j: (i, j, 0)), full(mu), full(w2p), full(a2p), full(g2p),
                  full(vec), full(ones)],
        out_specs=[tile] * 8,
        out_shape=[jax.ShapeDtypeStruct((b, s, dr), F32)] * 8,
        scratch_shapes=[pltpu.VMEM((8, n), F32)],
        compiler_params=_params(2), name="rwkv_prep")(p, mu, w2p, a2p, g2p, vec, ones)


def _cumsum_rows(x):
    n = x.shape[0]
    row = lax.broadcasted_iota(jnp.int32, x.shape, 0)
    d = 1
    while d < n:
        x = x + jnp.where(row >= d, pltpu.roll(x, d, axis=0), 0.0)
        d *= 2
    return x


def _split_bf16(x):
    hi = x.astype(BF16)
    return hi, (x - hi.astype(F32)).astype(BF16)


_NN = (((1,), (0,)), ((), ()))
_NT = (((1,), (1,)), ((), ()))


def _mm3(a, b, dims=_NN):
    ah, al = a if isinstance(a, tuple) else _split_bf16(a)
    bh, bl = b if isinstance(b, tuple) else _split_bf16(b)
    dg = lambda x, y: lax.dot_general(x, y, dims, preferred_element_type=F32)
    return dg(ah, bh) + (dg(ah, bl) + dg(al, bh))


def _each(f, *lists):
    return [f(*xs) for xs in zip(*lists)]


def _unit_lower_inverses(ns, row, col):
    eye = (row == col).astype(F32)
    size = ns[0].shape[0]
    n8 = [jnp.where((row >> 3) == (col >> 3), n, 0.0) for n in ns]
    n8s = _each(_split_bf16, n8)
    n8_2 = _each(lambda a: _mm3(a, a), n8s)
    n8_2s = _each(_split_bf16, n8_2)
    n8_4 = _each(lambda a: _mm3(a, a), n8_2s)
    p1 = _each(lambda n, n2, a, a2: eye + n + n2 + _mm3(a, a2), n8, n8_2, n8s, n8_2s)
    t = _each(lambda p, n4: p + _mm3(p, n4), p1, n8_4)
    sh = 4
    while (1 << (sh - 1)) < size:
        off = ((row >> sh) == (col >> sh)) & ((row >> (sh - 1)) != (col >> (sh - 1)))
        ts = _each(_split_bf16, t)
        tc = _each(lambda a, n: _mm3(a, jnp.where(off, n, 0.0)), ts, ns)
        t = _each(lambda x, y, a: x + _mm3(y, a), t, tc, ts)
        sh += 1
    return t


def _rw_scan_body(r_ref, lw_ref, k_ref, v_ref, kk_ref, b_ref, o_ref, h_ref, *, n_chunks):
    @pl.when(pl.program_id(2) == 0)
    def _():
        h_ref[...] = jnp.zeros_like(h_ref)

    cs = RW_CHUNK
    row = lax.broadcasted_iota(jnp.int32, (cs, cs), 0)
    col = lax.broadcasted_iota(jnp.int32, (cs, cs), 1)
    eye = (row == col).astype(F32)
    cat0 = lambda *xs: jnp.concatenate(xs, axis=0)
    cat1 = lambda *xs: jnp.concatenate(xs, axis=1)
    sls = [slice(c * cs, (c + 1) * cs) for c in range(n_chunks)]
    r, lw, k, v, kk, beta = ([ref[0, 0, sl, :] for sl in sls] for ref in (r_ref, lw_ref, k_ref, v_ref, kk_ref, b_ref))

    cum = _each(_cumsum_rows, lw)
    tot = [x[cs - 1:cs, :] for x in cum]
    a_t = _each(lambda kk_, c_, l_: -kk_ * jnp.exp(c_ - l_), kk, cum, lw)
    r_t = _each(lambda r_, c_: r_ * jnp.exp(c_), r, cum)
    b_t = _each(lambda b_, c_: b_ * jnp.exp(-c_), beta, cum)
    k_t = _each(lambda k_, c_: k_ * jnp.exp(-c_), k, cum)
    b_hT = _each(lambda b_, t_, c_: (b_ * jnp.exp(t_ - c_)).T, beta, tot, cum)
    k_hT = _each(lambda k_, t_, c_: (k_ * jnp.exp(t_ - c_)).T, k, tot, cum)
    vs = _each(_split_bf16, v)
    gram = _each(lambda a_, r_, b_, k_: _mm3(cat0(a_, r_), cat0(b_, k_), _NT), a_t, r_t, b_t, k_t)
    a_ab = [jnp.where(row > col, x[:cs, :cs], 0.0) for x in gram]
    a_ak = [jnp.where(row > col, x[:cs, cs:], 0.0) for x in gram]
    m_rb = [jnp.where(row >= col, x[cs:, :cs], 0.0) for x in gram]
    m_rk = [jnp.where(row >= col, x[cs:, cs:], 0.0) for x in gram]
    t_inv = _unit_lower_inverses(a_ab, row, col)
    akv = _each(_mm3, a_ak, vs)
    rkv = _each(_mm3, m_rk, vs)
    khv = _each(_mm3, k_hT, vs)
    wus = _each(lambda t_, a_, x_: _split_bf16(_mm3(t_, cat1(a_, x_))), t_inv, a_t, akv)
    qo = _each(lambda m_, w_, r_, x_: _mm3(m_, w_) + cat1(r_, x_), m_rb, wus, r_t, rkv)
    pd = _each(lambda b_, w_, t_, x_: _mm3(b_, w_) + cat1(eye * jnp.exp(t_), x_), b_hT, wus, tot, khv)

    h = h_ref[...]
    for c in range(n_chunks):
        qp = _mm3(cat0(qo[c][:, :cs], pd[c][:, :cs]), h)
        o_ref[0, 0, sls[c], :] = qp[:cs] + qo[c][:, cs:]
        h = qp[cs:] + pd[c][:, cs:]
    h_ref[...] = h


def _rw_post_body(o_ref, bonus_ref, g_ref, vec_ref, ones_ref, out_ref):
    o = o_ref[0]
    avg = ones_ref[...] * (1.0 / HEAD_DIM)
    mean = _mm(o, avg, HI)
    ctr = o - mean
    var = _mm(ctr * ctr, avg, HI)
    y = ctr * lax.rsqrt(var + RW_LN_EPS) * vec_ref[0:1, :] + vec_ref[1:2, :]
    out_ref[0] = ((y + bonus_ref[0]) * g_ref[0]).astype(out_ref.dtype)


def _rwkv(p, mu, w0, w2, a0, a2, g2, k_k, k_a, r_k, lnx_w, lnx_b, tb=512, tm=512):
    b, s, _ = p.shape
    hh, d, dr = RWKV_HEADS, HEAD_DIM, D_RWKV
    r, lw, k, v, kk, beta, g, bonus = _rw_prep(p, mu, w0, w2, a0, a2, g2, k_k, k_a, r_k)
    heads = lambda t: jnp.transpose(t.reshape(b, s, hh, d), (0, 2, 1, 3))
    tile = pl.BlockSpec((1, 1, tb, d), lambda i, j, c: (i, j, c, 0))
    o = pl.pallas_call(
        functools.partial(_rw_scan_body, n_chunks=tb // RW_CHUNK),
        grid=(b, hh, s // tb),
        in_specs=[tile] * 6,
        out_specs=tile,
        out_shape=jax.ShapeDtypeStruct((b, hh, s, d), F32),
        scratch_shapes=[pltpu.VMEM((d, d), F32)],
        compiler_params=_params(3), name="rwkv_scan")(*(heads(t) for t in (r, lw, k, v, kk, beta)))
    o = jnp.transpose(o, (0, 2, 1, 3)).reshape(b, s, dr)
    vec = jnp.concatenate([jnp.stack([lnx_w, lnx_b]), jnp.zeros((6, dr), F32)])
    ones = _head_ones()
    tile2 = pl.BlockSpec((1, tm, dr), lambda i, j: (i, j, 0))
    full = lambda arr: pl.BlockSpec(arr.shape, lambda i, j: (0, 0))
    return pl.pallas_call(
        _rw_post_body,
        grid=(b, s // tm),
        in_specs=[tile2, tile2, tile2, full(vec), full(ones)],
        out_specs=tile2,
        out_shape=jax.ShapeDtypeStruct((b, s, dr), BF16),
        compiler_params=_params(2), name="rwkv_post")(o, bonus, g, vec, ones)


def kernel(x, attn_norm, w_in, nsa_cmp_pos, nsa_cmp_w1, nsa_cmp_w2, nsa_out_gain, rw_mu, rw_w0, rw_w2, rw_a0, rw_a2, rw_g2, rw_k_k, rw_k_a, rw_r_k, rw_lnx_w, rw_lnx_b, moba_out_gain, w_out, ffn_norm, ffn_w_in, ffn_conv_w, ffn_conv_b, ffn_w_out, final_norm):
    b, s, d = x.shape
    depth = w_in.shape[0]
    d_ff = ffn_w_out.shape[1]
    w_in_p = jnp.concatenate([w_in[:, :, :N_IN_NSA], jnp.zeros((depth, d, GATE_PAD), w_in.dtype),
                              w_in[:, :, N_IN_NSA:]], axis=-1).astype(BF16)
    w_out_b = w_out.astype(BF16)
    ffn_w_in_b = ffn_w_in.astype(BF16)
    ffn_w_out_b = ffn_w_out.astype(BF16)
    for l in range(depth):
        proj = _norm_proj(x.reshape(b * s, d), attn_norm[l], w_in_p[l]).reshape(b, s, N_PROJ)
        o_nsa = _nsa(proj[..., :OFF_RW], nsa_cmp_pos[l], nsa_cmp_w1[l], nsa_cmp_w2[l], nsa_out_gain[l])
        o_rw = _rwkv(proj[..., OFF_RW:OFF_MOBA], rw_mu[l], rw_w0[l], rw_w2[l], rw_a0[l], rw_a2[l], rw_g2[l],
                     rw_k_k[l], rw_k_a[l], rw_r_k[l], rw_lnx_w[l], rw_lnx_b[l])
        pm = proj[..., OFF_MOBA:]
        o_moba = _moba(pm[..., :D_MOBA], pm[..., D_MOBA:2 * D_MOBA], pm[..., 2 * D_MOBA:], moba_out_gain[l])
        x = _out_proj(x.reshape(b * s, d), o_nsa.reshape(b * s, D_NSA), o_rw.reshape(b * s, D_RWKV),
                      o_moba.reshape(b * s, D_MOBA), w_out_b[l, :D_NSA], w_out_b[l, D_NSA:D_NSA + D_RWKV],
                      w_out_b[l, D_NSA + D_RWKV:]).reshape(b, s, d)
        x = _ffn(x, ffn_norm[l], ffn_w_in_b[l, :, :d_ff], ffn_w_in_b[l, :, d_ff:], ffn_conv_w[l], ffn_conv_b[l],
                 ffn_w_out_b[l], final_norm, final_norm=(l == depth - 1))
    return x
```

```python
import functools

import numpy as np
import jax
import jax.numpy as jnp
from jax import lax
from jax.experimental import pallas as pl
from jax.experimental.pallas import tpu as pltpu

HEAD_DIM = 64
NSA_HEADS = 6
NSA_KV_HEADS = 2
NSA_GROUP = NSA_HEADS // NSA_KV_HEADS
RWKV_HEADS = 4
MOBA_HEADS = 6
D_NSA = NSA_HEADS * HEAD_DIM
D_NSA_KV = NSA_KV_HEADS * HEAD_DIM
D_RWKV = RWKV_HEADS * HEAD_DIM
D_MOBA = MOBA_HEADS * HEAD_DIM
CMP_LEN = 32
CMP_STRIDE = 16
SLC_BLOCK = 64
SLC_TOPK = 16
WINDOW = 512
N_BRANCH = 3
RW_DECAY_SCALE = 0.606531
RW_LN_EPS = 64e-5
RW_LORA = 128
MOBA_BLOCK = 256
MOBA_TOPK = 3
CONV_WIDTH = 3
NORM_EPS = 1e-6
BIG = 1e9

N_GATE = NSA_HEADS * N_BRANCH
GATE_ROWS = 16
N_IN_RWKV = 3 * D_RWKV + RW_LORA
D_NSA_PAD = 256

NEG = -1e30
SCALE = HEAD_DIM ** -0.5
SCALE_LOG2E = SCALE * float(np.log2(np.e))
NSA_TQ = 128
NSA_TK = 256
RW_CHUNK = 64
PROJ_TM = 512
VMEM_LIMIT = 56 * 1024 * 1024

F32 = jnp.float32
BF16 = jnp.bfloat16
HI = lax.Precision.HIGHEST

_NN = (((1,), (0,)), ((), ()))
_NT = (((1,), (1,)), ((), ()))


def _params(n_axes):
    return pltpu.CompilerParams(dimension_semantics=("arbitrary",) * n_axes, vmem_limit_bytes=VMEM_LIMIT)


def _mm(a, b, precision=None):
    return jnp.dot(a, b, preferred_element_type=F32, precision=precision)


def _rms(x, gain):
    return x * lax.rsqrt(jnp.mean(x * x, axis=-1, keepdims=True) + NORM_EPS) * gain


def _split_bf16(x):
    hi = x.astype(BF16)
    return hi, (x - hi.astype(F32)).astype(BF16)


def _mm3(a, b, dims=_NN):
    ah, al = a if isinstance(a, tuple) else _split_bf16(a)
    bh, bl = b if isinstance(b, tuple) else _split_bf16(b)
    dg = lambda x, y: lax.dot_general(x, y, dims, preferred_element_type=F32)
    return dg(ah, bh) + (dg(ah, bl) + dg(al, bh))


def _mm2(a, b):
    ah, al = _split_bf16(a)
    return _mm(ah, b) + _mm(al, b)


def _each(f, *lists):
    return [f(*xs) for xs in zip(*lists)]


_NAT = dict(kc=(0, 128), vc=(128, 256), ks=(256, 384), kw=(384, 512), rw=(512, 512 + N_IN_RWKV),
            mk=(512 + N_IN_RWKV, 512 + N_IN_RWKV + D_MOBA))
N_NAT = 512 + N_IN_RWKV + D_MOBA
_TR = dict(qn=(0, D_NSA), vs=(D_NSA, D_NSA + 128), vw=(D_NSA + 128, D_NSA + 256),
           gt=(D_NSA + 256, D_NSA + 256 + NSA_KV_HEADS * GATE_ROWS))
_TR['mq'] = (_TR['gt'][1], _TR['gt'][1] + D_MOBA)
_TR['mv'] = (_TR['mq'][1], _TR['mq'][1] + D_MOBA)
N_TR = _TR['mv'][1]


def _norm_proj_body(x_ref, g_ref, wn_ref, wt_ref, kv_ref, ks_ref, kw_ref, rw_ref, mk_ref, qn_ref, vs_ref, vw_ref,
                    gt_ref, mq_ref, mv_ref):
    h = _rms(x_ref[0], g_ref[...]).astype(BF16)
    tm = h.shape[0]
    nat = _mm(h, wn_ref[...])
    tr = lax.dot_general(wt_ref[...], h, _NT, preferred_element_type=F32)
    cut = lambda name: nat[:, _NAT[name][0]:_NAT[name][1]]
    rows = lambda name: tr[_TR[name][0]:_TR[name][1], :]
    kv_ref[0, 0] = cut('kc')
    kv_ref[1, 0] = cut('vc')
    ks_ref[0] = cut('ks').astype(BF16).reshape(tm // NSA_TK, NSA_TK, D_NSA_KV)
    kw_ref[0] = cut('kw').astype(BF16).reshape(tm // NSA_TQ, NSA_TQ, D_NSA_KV)
    rw_ref[0] = cut('rw')
    mk_ref[0] = cut('mk').astype(BF16).reshape(tm // MOBA_BLOCK, MOBA_BLOCK, D_MOBA)
    qn_ref[0] = rows('qn').astype(BF16)
    gt_ref[0] = rows('gt')
    for name, ref, width in (('vs', vs_ref, NSA_TK), ('vw', vw_ref, NSA_TQ), ('mq', mq_ref, MOBA_BLOCK),
                             ('mv', mv_ref, MOBA_BLOCK)):
        t = rows(name).astype(BF16)
        for i in range(tm // width):
            ref[0, i] = t[:, i * width:(i + 1) * width]


def _proj_weights(w_in):
    kvw = D_NSA_KV
    o = D_NSA
    q_nsa = w_in[:, :o]
    kc, vc, ks, vs, kw, vw = (w_in[:, o + i * kvw:o + (i + 1) * kvw] for i in range(6))
    o += 6 * kvw
    gates = w_in[:, o:o + N_GATE]
    o += N_GATE
    rw = w_in[:, o:o + N_IN_RWKV]
    o += N_IN_RWKV
    mq, mk, mv = (w_in[:, o + i * D_MOBA:o + (i + 1) * D_MOBA] for i in range(3))
    src = np.zeros((NSA_KV_HEADS * GATE_ROWS,), np.int32)
    used = np.zeros((NSA_KV_HEADS * GATE_ROWS,), np.float32)
    for g in range(NSA_KV_HEADS):
        for r in range(NSA_GROUP):
            for br in range(N_BRANCH):
                src[g * GATE_ROWS + br * NSA_GROUP + r] = (g * NSA_GROUP + r) * N_BRANCH + br
                used[g * GATE_ROWS + br * NSA_GROUP + r] = 1.0
    gates_t = gates[:, src] * used[None, :]
    wn = jnp.concatenate([kc, vc, ks, kw, rw, mk], axis=1).astype(BF16)
    wt = jnp.concatenate([q_nsa, vs, vw, gates_t, mq, mv], axis=1).T.astype(BF16)
    return wn, wt


def _norm_proj(x, gain, wn, wt, tm=PROJ_TM):
    b, s, d = x.shape
    tok = lambda w, dt: (jax.ShapeDtypeStruct((b, s, w), dt), pl.BlockSpec((1, tm, w), lambda i, j: (i, j, 0)))
    tiles = lambda t, shape, dt: (jax.ShapeDtypeStruct((b, s // t) + shape, dt),
                                  pl.BlockSpec((1, tm // t) + shape, lambda i, j: (i, j, 0, 0)))
    lanes = lambda r, dt: (jax.ShapeDtypeStruct((b, r, s), dt), pl.BlockSpec((1, r, tm), lambda i, j: (i, 0, j)))
    outs = [
        (jax.ShapeDtypeStruct((2, b, s, D_NSA_KV), F32), pl.BlockSpec((2, 1, tm, D_NSA_KV), lambda i, j: (0, i, j, 0))),
        tiles(NSA_TK, (NSA_TK, D_NSA_KV), BF16),
        tiles(NSA_TQ, (NSA_TQ, D_NSA_KV), BF16),
        tok(N_IN_RWKV, F32),
        tiles(MOBA_BLOCK, (MOBA_BLOCK, D_MOBA), BF16),
        lanes(D_NSA, BF16),
        tiles(NSA_TK, (D_NSA_KV, NSA_TK), BF16),
        tiles(NSA_TQ, (D_NSA_KV, NSA_TQ), BF16),
        lanes(NSA_KV_HEADS * GATE_ROWS, F32),
        tiles(MOBA_BLOCK, (D_MOBA, MOBA_BLOCK), BF16),
        tiles(MOBA_BLOCK, (D_MOBA, MOBA_BLOCK), BF16),
    ]
    full = lambda arr: pl.BlockSpec(arr.shape, lambda i, j: (0,) * arr.ndim)
    gain = gain.reshape(1, d)
    return pl.pallas_call(
        _norm_proj_body,
        grid=(b, s // tm),
        in_specs=[pl.BlockSpec((1, tm, d), lambda i, j: (i, j, 0)), full(gain), full(wn), full(wt)],
        out_specs=[o[1] for o in outs],
        out_shape=[o[0] for o in outs],
        compiler_params=_params(2), name="norm_proj")(x, gain, wn, wt)


def _out_proj_body(x_ref, a_ref, b_ref, c_ref, wa_ref, wb_ref, wc_ref, o_ref):
    o_ref[...] = (x_ref[...] + _mm(a_ref[...], wa_ref[...]) + _mm(b_ref[...], wb_ref[...])
                  + _mm(c_ref[...], wc_ref[...]))


def _out_proj(x2d, a, b, c, wa, wb, wc, tm=512):
    m, d = x2d.shape
    row = lambda w: pl.BlockSpec((tm, w), lambda i: (i, 0))
    full = lambda arr: pl.BlockSpec(arr.shape, lambda i: (0, 0))
    return pl.pallas_call(
        _out_proj_body,
        grid=(m // tm,),
        in_specs=[row(d), row(a.shape[1]), row(b.shape[1]), row(c.shape[1]), full(wa), full(wb), full(wc)],
        out_specs=row(d),
        out_shape=jax.ShapeDtypeStruct((m, d), F32),
        compiler_params=_params(1), name="out_proj")(x2d, a, b, c, wa, wb, wc)


def _ffn_body(x_ref, g_ref, wu_ref, wg_ref, cw_ref, cb_ref, wo_ref, fg_ref, o_ref, carry_ref, *, tf, final_norm):
    @pl.when(pl.program_id(1) == 0)
    def _():
        carry_ref[...] = jnp.zeros_like(carry_ref)

    x = x_ref[0]
    tm = x.shape[0]
    h = _rms(x, g_ref[...]).astype(BF16)
    row = lax.broadcasted_iota(jnp.int32, (tm, tf), 0)
    acc = x
    for c in range(wu_ref.shape[1] // tf):
        cs = slice(c * tf, (c + 1) * tf)
        u = _mm(h, wu_ref[:, cs])
        g = _mm(h, wg_ref[:, cs])
        prev = carry_ref[:, cs]
        g1 = jnp.where(row == 0, prev[7:8, :], pltpu.roll(g, 1, axis=0))
        g2 = jnp.where(row == 0, prev[6:7, :], jnp.where(row == 1, prev[7:8, :], pltpu.roll(g, 2, axis=0)))
        carry_ref[:, cs] = g[tm - 8:, :]
        cw = cw_ref[:, cs]
        gc = cw[0:1, :] * g2 + cw[1:2, :] * g1 + cw[2:3, :] * g + cb_ref[:, cs]
        act = (gc * jax.nn.sigmoid(gc) * u).astype(BF16)
        acc = acc + _mm(act, wo_ref[cs, :])
    if final_norm:
        acc = _rms(acc, fg_ref[...])
    o_ref[0] = acc


def _ffn(x, gain, wu, wg, cw, cb, wo, fgain, final_norm, tm=256, tf=1408):
    b, s, d = x.shape
    dff = wu.shape[1]
    full = lambda arr: pl.BlockSpec(arr.shape, lambda i, j: (0, 0))
    gain = gain.reshape(1, d)
    cb = cb.reshape(1, dff)
    fgain = fgain.reshape(1, d)
    return pl.pallas_call(
        functools.partial(_ffn_body, tf=tf, final_norm=final_norm),
        grid=(b, s // tm),
        in_specs=[pl.BlockSpec((1, tm, d), lambda i, j: (i, j, 0)), full(gain), full(wu), full(wg), full(cw),
                  full(cb), full(wo), full(fgain)],
        out_specs=pl.BlockSpec((1, tm, d), lambda i, j: (i, j, 0)),
        out_shape=jax.ShapeDtypeStruct((b, s, d), F32),
        scratch_shapes=[pltpu.VMEM((8, dff), F32)],
        compiler_params=_params(2), name="conv_glu")(x, gain, wu, wg, cw, cb, wo, fgain)


def _fold8(x, op):
    return op(x.reshape(x.shape[0] // 8, 8, x.shape[1]), axis=0)


UNROLLS = (4, 2, 1)


def _grouped_loop(n, body, init):
    start, carry = 0, init
    for width in UNROLLS:
        count = (n - start) // width

        def group(i, c, width=width, start=start):
            for u in range(width):
                c = body(start + i * width + u, c)
            return c

        carry = lax.fori_loop(0, count, group, carry)
        start = start + count * width
    return carry


def _pad_heads(q, slot, n_slots):
    zero = jnp.zeros_like(q)
    if isinstance(slot, int):
        parts = [q if i == slot else zero for i in range(n_slots)]
    else:
        parts = [jnp.where(slot == i, q, zero) for i in range(n_slots)]
    return jnp.concatenate(parts, axis=0)


def _moba_body(q_ref, k_ref, vT_ref, gain_ref, o_ref, kmean_ref, bias_ref, s_ref, *, nb):
    qi = pl.program_id(2)
    blk, d = MOBA_BLOCK, HEAD_DIM
    pair = range(2)

    @pl.when(qi == 0)
    def _():
        kmean_ref[...] = jnp.mean(k_ref[0].astype(F32), axis=1)

    q2 = q_ref[0, 0]
    qp = [_pad_heads(q2[h * d:(h + 1) * d], h, 2) for h in pair]
    km_hi, km_lo = _split_bf16(kmean_ref[...])
    bid = lax.broadcasted_iota(jnp.int32, (nb, blk), 0)
    for h in pair:
        gate = _mm(km_hi, qp[h]) + _mm(km_lo, qp[h])
        gate = jnp.where(bid < qi, gate, -jnp.inf)
        sel = jnp.zeros((nb, blk), F32)
        for _ in range(min(MOBA_TOPK, max(nb - 1, 1))):
            m = jnp.max(gate, axis=0, keepdims=True)
            cand = (gate == m) & (m > -jnp.inf)
            idx = jnp.min(jnp.where(cand, bid, nb), axis=0, keepdims=True)
            hit = bid == idx
            sel = jnp.where(hit, 1.0, sel)
            gate = jnp.where(hit, -jnp.inf, gate)
        bias_ref[h] = jnp.where(sel > 0.5, 0.0, NEG)

    kpos = lax.broadcasted_iota(jnp.int32, (blk, blk), 0)
    qpos = lax.broadcasted_iota(jnp.int32, (blk, blk), 1)
    k_own = k_ref[0, qi]
    m8 = []
    for h in pair:
        s = jnp.where(kpos <= qpos, _mm(k_own, qp[h]) * SCALE_LOG2E, NEG)
        s_ref[h, qi] = s
        m8.append(_fold8(s, jnp.max))

    def score(j, m8):
        kj = k_ref[0, j]
        out = []
        for h in pair:
            s = _mm(kj, qp[h]) * SCALE_LOG2E + bias_ref[h, pl.ds(j, 1), :]
            s_ref[h, j] = s
            out.append(jnp.maximum(m8[h], _fold8(s, jnp.max)))
        return tuple(out)

    m8 = _grouped_loop(qi, score, tuple(m8))
    m = [jnp.max(x, axis=0, keepdims=True) for x in m8]

    def accum(j, carry):
        vj = vT_ref[0, j]
        out = []
        for h in pair:
            l8, acc = carry[h]
            p = jnp.exp2(s_ref[h, j] - m[h])
            out.append((l8 + _fold8(p, jnp.sum), acc + _mm(vj[h * d:(h + 1) * d], p.astype(BF16))))
        return tuple(out)

    zero = (jnp.zeros((8, blk), F32), jnp.zeros((d, blk), F32))
    res = _grouped_loop(qi + 1, accum, (zero, zero))
    outs = []
    for h in pair:
        l8, acc = res[h]
        o = acc / jnp.maximum(jnp.sum(l8, axis=0, keepdims=True), 1e-30)
        outs.append(o * lax.rsqrt(jnp.mean(o * o, axis=0, keepdims=True) + NORM_EPS) * gain_ref[0, h])
    o_ref[0] = jnp.concatenate(outs, axis=0).T.astype(o_ref.dtype)


def _moba(qT, kb, vT, gain):
    b, nb, dm, blk = qT.shape
    d2 = 2 * HEAD_DIM
    return pl.pallas_call(
        functools.partial(_moba_body, nb=nb),
        grid=(b, dm // d2, nb),
        in_specs=[pl.BlockSpec((1, 1, d2, blk), lambda i, p, c: (i, c, p, 0)),
                  pl.BlockSpec((1, nb, blk, d2), lambda i, p, c: (i, 0, 0, p)),
                  pl.BlockSpec((1, nb, d2, blk), lambda i, p, c: (i, 0, p, 0)),
                  pl.BlockSpec((1, 2, HEAD_DIM, 1), lambda i, p, c: (p, 0, 0, 0))],
        out_specs=pl.BlockSpec((1, blk, d2), lambda i, p, c: (i, c, p)),
        out_shape=jax.ShapeDtypeStruct((b, nb * blk, dm), BF16),
        scratch_shapes=[pltpu.VMEM((nb, d2), F32), pltpu.VMEM((2, nb, blk), F32),
                        pltpu.VMEM((2, nb, blk, blk), F32)],
        compiler_params=_params(3), name="moba")(qT, kb, vT, gain.reshape(dm // d2, 2, HEAD_DIM, 1))


def _gelu_tanh(x):
    return x * (0.5 * (1.0 + jnp.tanh(np.sqrt(2.0 / np.pi) * (x + 0.044715 * (x * x * x)))))


def _nsa_cmp_body(r_ref, pos_ref, wtop_ref, wbot_ref, w2_ref, o_ref, oT_ref):
    r = r_ref[0, 0]
    nc = r.shape[0]
    y = _mm(r + pos_ref[0, 0], wtop_ref[0], HI)
    z = _mm(r + pos_ref[0, 1], wbot_ref[0], HI)
    pre = y + pltpu.roll(z, nc - 1, axis=0)
    o = _mm(_gelu_tanh(pre), w2_ref[0], HI)
    o_ref[0, 0] = o
    oT_ref[0, 0] = o.T


def _nsa_compress(kv, pos, w1, w2):
    _, b, s, _ = kv.shape
    g, d = NSA_KV_HEADS, HEAD_DIM
    nc = s // CMP_STRIDE
    hid = w1.shape[-1]
    half = CMP_LEN // 2
    eye = jnp.eye(g, dtype=F32)
    w1r = w1.reshape(2, 2, half, d, hid)
    w1p = jnp.einsum('thjdc,gk->thjgdkc', w1r, eye).reshape(2, 2, half * g * d, g * hid)
    posp = jnp.broadcast_to(pos.reshape(2, 2, half, 1, d), (2, 2, half, g, d)).reshape(2, 2, 1, half * g * d)
    w2p = jnp.einsum('tcd,gk->tgckd', w2, eye).reshape(2, g * hid, g * d)
    kd = half * g * d
    r = kv.reshape(2, b, nc, kd)
    return pl.pallas_call(
        _nsa_cmp_body,
        grid=(2, b),
        in_specs=[pl.BlockSpec((1, 1, nc, kd), lambda t, i: (t, i, 0, 0)),
                  pl.BlockSpec((1, 2, 1, kd), lambda t, i: (t, 0, 0, 0)),
                  pl.BlockSpec((1, kd, g * hid), lambda t, i: (t, 0, 0)),
                  pl.BlockSpec((1, kd, g * hid), lambda t, i: (t, 0, 0)),
                  pl.BlockSpec((1, g * hid, g * d), lambda t, i: (t, 0, 0))],
        out_specs=[pl.BlockSpec((1, 1, nc, g * d), lambda t, i: (t, i, 0, 0)),
                   pl.BlockSpec((1, 1, g * d, nc), lambda t, i: (t, i, 0, 0))],
        out_shape=[jax.ShapeDtypeStruct((2, b, nc, g * d), F32), jax.ShapeDtypeStruct((2, b, g * d, nc), F32)],
        compiler_params=_params(2), name="nsa_compress")(r, posp, w1p[:, 0], w1p[:, 1], w2p)


def _cmp_to_slc_T(nc, n_slc):
    r = SLC_BLOCK // CMP_STRIDE
    c = CMP_LEN // CMP_STRIDE
    i = (r * np.arange(n_slc)[:, None, None] - np.arange(r)[None, :, None] - np.arange(c)[None, None, :]).reshape(n_slc, -1)
    m = (i[:, :, None] == np.arange(nc - 1)[None, None, :]).sum(1)
    return np.concatenate([m, np.zeros((n_slc, 1), m.dtype)], axis=1).astype(np.float32)


def _nsa_attn_body(q_ref, gt_ref, kc_ref, vcT_ref, map_ref, ks_ref, vsT_ref, kw_ref, vwT_ref, gain_ref, o_ref,
                   bias_ref, s_ref, *, nc, n_slc):
    grp = pl.program_id(1)
    c = pl.program_id(2)
    tq, rr, d = NSA_TQ, NSA_GROUP, HEAD_DIM
    nl = tq * rr
    t0 = c * tq
    q3 = q_ref[0]
    q = jnp.concatenate([q3[r * d:(r + 1) * d] for r in range(rr)], axis=1)
    qp = _pad_heads(q, grp, NSA_KV_HEADS)
    lane = lax.broadcasted_iota(jnp.int32, (1, nl), 1)
    tpos3 = t0 + (lane & (tq - 1))

    sc = _mm2(kc_ref[0, 0], qp) * SCALE
    n_id = lax.broadcasted_iota(jnp.int32, (nc, nl), 0)
    cmask = (n_id * CMP_STRIDE + (CMP_LEN - 1)) <= tpos3
    scm = jnp.where(cmask, sc, NEG)
    pc = jnp.where(cmask, jnp.exp(scm - jnp.max(scm, axis=0, keepdims=True)), 0.0)
    pc = pc / jnp.maximum(jnp.sum(pc, axis=0, keepdims=True), 1e-30)
    oc = _mm(vcT_ref[0, 0].astype(BF16), pc.astype(BF16))

    pcs = pc[:, 0:tq]
    for r in range(1, rr):
        pcs = pcs + pc[:, r * tq:(r + 1) * tq]
    p_hi, p_lo = _split_bf16(pcs)
    imp = _mm(map_ref[...], p_hi) + _mm(map_ref[...], p_lo)
    bid = lax.broadcasted_iota(jnp.int32, (n_slc, tq), 0)
    tpos = t0 + lax.broadcasted_iota(jnp.int32, (1, tq), 1)
    cur = tpos // SLC_BLOCK
    forced = (bid == 0) | (bid == cur) | (bid == cur - 1)
    val = jnp.where(forced, BIG, jnp.where(bid * SLC_BLOCK <= tpos, imp, -BIG))
    sel = jnp.zeros((n_slc, tq), F32)
    for _ in range(min(SLC_TOPK, n_slc)):
        m = jnp.max(val, axis=0, keepdims=True)
        idx = jnp.min(jnp.where(val == m, bid, n_slc), axis=0, keepdims=True)
        hit = bid == idx
        sel = jnp.where(hit, 1.0, sel)
        val = jnp.where(hit, -jnp.inf, val)
    bias_ref[...] = jnp.where(sel > 0.5, 0.0, NEG)

    per_tile = NSA_TK // SLC_BLOCK

    def sel_scores(j, diagonal):
        s = _mm(ks_ref[0, j], qp) * SCALE_LOG2E
        parts = []
        for i in range(per_tile):
            brow = bias_ref[pl.ds(per_tile * j + i, 1), :]
            parts.append(s[i * SLC_BLOCK:(i + 1) * SLC_BLOCK, :] + jnp.concatenate([brow] * rr, axis=1))
        s = jnp.concatenate(parts, axis=0)
        if diagonal:
            kpos = j * NSA_TK + lax.broadcasted_iota(jnp.int32, (NSA_TK, nl), 0)
            s = jnp.where(kpos <= tpos3, s, NEG)
        s_ref[j] = s
        return _fold8(s, jnp.max)

    jl = t0 // NSA_TK
    m8 = _grouped_loop(jl, lambda j, m8: jnp.maximum(m8, sel_scores(j, False)), sel_scores(jl, True))
    m_s = jnp.max(m8, axis=0, keepdims=True)

    def sel_accum(j, carry):
        l8, acc = carry
        p = jnp.exp2(s_ref[j] - m_s)
        return l8 + _fold8(p, jnp.sum), acc + _mm(vsT_ref[0, j], p.astype(BF16))

    l8, acc_s = _grouped_loop(jl + 1, sel_accum, (jnp.zeros((8, nl), F32), jnp.zeros((d, nl), F32)))
    o_s = acc_s / jnp.maximum(jnp.sum(l8, axis=0, keepdims=True), 1e-30)

    n_w = WINDOW // tq
    w_tiles = []
    for i in range(n_w + 1):
        widx = c - n_w + i
        wcl = jnp.maximum(widx, 0)
        s = _mm(kw_ref[0, wcl], qp) * SCALE_LOG2E
        kpos = widx * tq + lax.broadcasted_iota(jnp.int32, (tq, nl), 0)
        dist = tpos3 - kpos
        w_tiles.append((jnp.where((dist >= 0) & (dist < WINDOW) & (kpos >= 0), s, NEG), wcl))
    m8 = functools.reduce(jnp.maximum, [_fold8(s, jnp.max) for s, _ in w_tiles])
    m_w = jnp.max(m8, axis=0, keepdims=True)
    l8, acc_w = jnp.zeros((8, nl), F32), jnp.zeros((d, nl), F32)
    for s, wcl in w_tiles:
        p = jnp.exp2(s - m_w)
        l8, acc_w = l8 + _fold8(p, jnp.sum), acc_w + _mm(vwT_ref[0, wcl], p.astype(BF16))
    o_w = acc_w / jnp.maximum(jnp.sum(l8, axis=0, keepdims=True), 1e-30)

    g = jax.nn.sigmoid(gt_ref[0])
    outs = []
    for r in range(rr):
        sl = slice(r * tq, (r + 1) * tq)
        o = g[r:r + 1, :] * oc[:, sl] + g[rr + r:rr + r + 1, :] * o_s[:, sl] + g[2 * rr + r:2 * rr + r + 1, :] * o_w[:, sl]
        o = o * lax.rsqrt(jnp.mean(o * o, axis=0, keepdims=True) + NORM_EPS) * gain_ref[0, r]
        outs.append(o)
    outs.append(jnp.zeros((D_NSA_PAD - rr * d, tq), F32))
    o_ref[0] = jnp.concatenate(outs, axis=0).T.astype(o_ref.dtype)


def _nsa(qn, gt, kv, ks, vsT, kw, vwT, cmp_pos, cmp_w1, cmp_w2, gain):
    b, _, s = qn.shape
    g, rr, d, tq = NSA_KV_HEADS, NSA_GROUP, HEAD_DIM, NSA_TQ
    nq, nl = s // tq, NSA_TQ * NSA_GROUP
    nc, n_slc = s // CMP_STRIDE, s // SLC_BLOCK
    cmp, cmpT = _nsa_compress(kv, cmp_pos, cmp_w1, cmp_w2)
    cmap = jnp.asarray(_cmp_to_slc_T(nc, n_slc), BF16)
    whole = lambda arr: pl.BlockSpec((1,) + arr.shape[1:], lambda i, j, c: (i, 0, 0, 0))
    return pl.pallas_call(
        functools.partial(_nsa_attn_body, nc=nc, n_slc=n_slc),
        grid=(b, g, nq),
        in_specs=[pl.BlockSpec((1, rr * d, tq), lambda i, j, c: (i, j, c)),
                  pl.BlockSpec((1, GATE_ROWS, tq), lambda i, j, c: (i, j, c)),
                  pl.BlockSpec((1, 1, nc, g * d), lambda i, j, c: (0, i, 0, 0)),
                  pl.BlockSpec((1, 1, d, nc), lambda i, j, c: (1, i, j, 0)),
                  pl.BlockSpec((n_slc, nc), lambda i, j, c: (0, 0)),
                  whole(ks),
                  pl.BlockSpec((1, s // NSA_TK, d, NSA_TK), lambda i, j, c: (i, 0, j, 0)),
                  whole(kw),
                  pl.BlockSpec((1, nq, d, tq), lambda i, j, c: (i, 0, j, 0)),
                  pl.BlockSpec((1, rr, d, 1), lambda i, j, c: (j, 0, 0, 0))],
        out_specs=pl.BlockSpec((1, tq, D_NSA_PAD), lambda i, j, c: (i, c, j)),
        out_shape=jax.ShapeDtypeStruct((b, s, g * D_NSA_PAD), BF16),
        scratch_shapes=[pltpu.VMEM((n_slc, tq), F32), pltpu.VMEM((s // NSA_TK, NSA_TK, nl), F32)],
        compiler_params=_params(3), name="nsa_attn")(qn, gt, cmp, cmpT, cmap, ks, vsT, kw, vwT,
                                                      gain.reshape(g, rr, d, 1))


def _head_ones():
    i = np.arange(D_RWKV) // HEAD_DIM
    return (i[:, None] == i[None, :]).astype(np.float32)


def _rw_prep_body(p_ref, mu_ref, w2_ref, a2_ref, g2_ref, vec_ref, ones_ref,
                  r_ref, lw_ref, k_ref, v_ref, kk_ref, b_ref, g_ref, bonus_ref, carry_ref):
    @pl.when(pl.program_id(1) == 0)
    def _():
        carry_ref[...] = jnp.zeros_like(carry_ref)

    p = p_ref[0]
    t = p.shape[0]
    row = lax.broadcasted_iota(jnp.int32, p.shape, 0)
    prev = jnp.where(row == 0, carry_ref[7:8, :], pltpu.roll(p, 1, axis=0))
    carry_ref[...] = p[t - 8:, :]
    xs = p + (prev - p) * mu_ref[...]
    dr = D_RWKV
    r, k, v, lora = xs[:, :dr], xs[:, dr:2 * dr], xs[:, 2 * dr:3 * dr], xs[:, 3 * dr:]
    w0, a0, k_k, k_a, r_k = (vec_ref[i:i + 1, :] for i in range(5))
    ones = ones_ref[...]
    logw = -RW_DECAY_SCALE * jax.nn.sigmoid(w0 + _mm(jnp.tanh(lora), w2_ref[...], HI))
    a = jax.nn.sigmoid(a0 + _mm(lora, a2_ref[...], HI))
    g_ref[0] = _mm(jax.nn.sigmoid(lora), g2_ref[...], HI)
    kk = k * k_k
    kk = kk / jnp.maximum(jnp.sqrt(_mm(kk * kk, ones, HI)), 1e-12)
    k = k * (1.0 + (a - 1.0) * k_a)
    r_ref[0] = r
    lw_ref[0] = logw
    k_ref[0] = k
    v_ref[0] = v
    kk_ref[0] = kk
    b_ref[0] = kk * a
    bonus_ref[0] = _mm(r * k * r_k, ones, HI) * v


def _rw_prep(p, mu, w0, w2, a0, a2, g2, k_k, k_a, r_k, tm=512):
    b, s, n = p.shape
    dr = D_RWKV
    nl = n - 3 * dr
    pad = lambda w, lo: jnp.zeros((nl, dr), F32).at[lo:lo + w.shape[0]].set(w)
    w2p, a2p, g2p = pad(w2, 0), pad(a2, w2.shape[0]), pad(g2, w2.shape[0] + a2.shape[0])
    vec = jnp.concatenate([jnp.stack([w0, a0, k_k, k_a, r_k.reshape(dr)]), jnp.zeros((3, dr), F32)])
    full = lambda arr: pl.BlockSpec(arr.shape, lambda i, j: (0, 0))
    ones = jnp.asarray(_head_ones())
    mu = mu.reshape(1, n)
    tile = pl.BlockSpec((1, tm, dr), lambda i, j: (i, j, 0))
    return pl.pallas_call(
        _rw_prep_body,
        grid=(b, s // tm),
        in_specs=[pl.BlockSpec((1, tm, n), lambda i, j: (i, j, 0)), full(mu), full(w2p), full(a2p), full(g2p),
                  full(vec), full(ones)],
        out_specs=[tile] * 8,
        out_shape=[jax.ShapeDtypeStruct((b, s, dr), F32)] * 8,
        scratch_shapes=[pltpu.VMEM((8, n), F32)],
        compiler_params=_params(2), name="rwkv_prep")(p, mu, w2p, a2p, g2p, vec, ones)


def _cumsum_rows(x):
    n = x.shape[0]
    row = lax.broadcasted_iota(jnp.int32, x.shape, 0)
    d = 1
    while d < n:
        x = x + jnp.where(row >= d, pltpu.roll(x, d, axis=0), 0.0)
        d *= 2
    return x


def _unit_lower_inverses(ns, row, col):
    eye = (row == col).astype(F32)
    size = ns[0].shape[0]
    n8 = [jnp.where((row >> 3) == (col >> 3), n, 0.0) for n in ns]
    n8s = _each(_split_bf16, n8)
    n8_2 = _each(lambda a: _mm3(a, a), n8s)
    n8_2s = _each(_split_bf16, n8_2)
    n8_4 = _each(lambda a: _mm3(a, a), n8_2s)
    p1 = _each(lambda n, n2, a, a2: eye + n + n2 + _mm3(a, a2), n8, n8_2, n8s, n8_2s)
    t = _each(lambda p, n4: p + _mm3(p, n4), p1, n8_4)
    sh = 4
    while (1 << (sh - 1)) < size:
        off = ((row >> sh) == (col >> sh)) & ((row >> (sh - 1)) != (col >> (sh - 1)))
        ts = _each(_split_bf16, t)
        tc = _each(lambda a, n: _mm3(a, jnp.where(off, n, 0.0)), ts, ns)
        t = _each(lambda x, y, a: x + _mm3(y, a), t, tc, ts)
        sh += 1
    return t


def _rw_scan_body(r_ref, lw_ref, k_ref, v_ref, kk_ref, b_ref, bonus_ref, g_ref, vec_ref, avg_ref, o_ref, h_ref,
                  *, n_chunks):
    @pl.when(pl.program_id(1) == 0)
    def _():
        h_ref[...] = jnp.zeros_like(h_ref)

    cs, d, nh = RW_CHUNK, HEAD_DIM, RWKV_HEADS
    row = lax.broadcasted_iota(jnp.int32, (cs, cs), 0)
    col = lax.broadcasted_iota(jnp.int32, (cs, cs), 1)
    eye = (row == col).astype(F32)
    cat0 = lambda *xs: jnp.concatenate(xs, axis=0)
    cat1 = lambda *xs: jnp.concatenate(xs, axis=1)
    units = [(c, h) for c in range(n_chunks) for h in range(nh)]
    take = lambda ref: [ref[0, c * cs:(c + 1) * cs, :][:, h * d:(h + 1) * d] for c, h in units]
    r, lw, k, v, kk, beta = (take(ref) for ref in (r_ref, lw_ref, k_ref, v_ref, kk_ref, b_ref))

    cum = _each(_cumsum_rows, lw)
    tot = [x[cs - 1:cs, :] for x in cum]
    a_t = _each(lambda kk_, c_, l_: -kk_ * jnp.exp(c_ - l_), kk, cum, lw)
    r_t = _each(lambda r_, c_: r_ * jnp.exp(c_), r, cum)
    b_t = _each(lambda b_, c_: b_ * jnp.exp(-c_), beta, cum)
    k_t = _each(lambda k_, c_: k_ * jnp.exp(-c_), k, cum)
    b_hT = _each(lambda b_, t_, c_: (b_ * jnp.exp(t_ - c_)).T, beta, tot, cum)
    k_hT = _each(lambda k_, t_, c_: (k_ * jnp.exp(t_ - c_)).T, k, tot, cum)
    vs = _each(_split_bf16, v)
    gram = _each(lambda a_, r_, b_, k_: _mm3(cat0(a_, r_), cat0(b_, k_), _NT), a_t, r_t, b_t, k_t)
    a_ab = [jnp.where(row > col, x[:cs, :cs], 0.0) for x in gram]
    a_ak = [jnp.where(row > col, x[:cs, cs:], 0.0) for x in gram]
    m_rb = [jnp.where(row >= col, x[cs:, :cs], 0.0) for x in gram]
    m_rk = [jnp.where(row >= col, x[cs:, cs:], 0.0) for x in gram]
    t_inv = _unit_lower_inverses(a_ab, row, col)
    akv = _each(_mm3, a_ak, vs)
    rkv = _each(_mm3, m_rk, vs)
    khv = _each(_mm3, k_hT, vs)
    wus = _each(lambda t_, a_, x_: _split_bf16(_mm3(t_, cat1(a_, x_))), t_inv, a_t, akv)
    qo = _each(lambda m_, w_, r_, x_: _mm3(m_, w_) + cat1(r_, x_), m_rb, wus, r_t, rkv)
    pd = _each(lambda b_, w_, t_, x_: _mm3(b_, w_) + cat1(eye * jnp.exp(t_), x_), b_hT, wus, tot, khv)

    hs = [h_ref[h] for h in range(nh)]
    outs = []
    for c in range(n_chunks):
        heads = []
        for h in range(nh):
            u = c * nh + h
            qp = _mm3(cat0(qo[u][:, :cs], pd[u][:, :cs]), hs[h])
            heads.append(qp[:cs] + qo[u][:, cs:])
            hs[h] = qp[cs:] + pd[u][:, cs:]
        outs.append(cat1(*heads))
    for h in range(nh):
        h_ref[h] = hs[h]

    o = cat0(*outs)
    avg = avg_ref[...]
    mean = _mm2(o, avg)
    ctr = o - mean
    var = _mm2(ctr * ctr, avg)
    y = ctr * lax.rsqrt(var + RW_LN_EPS) * vec_ref[0:1, :] + vec_ref[1:2, :]
    o_ref[0] = ((y + bonus_ref[0]) * g_ref[0]).astype(o_ref.dtype)


def _rwkv(p, mu, w0, w2, a0, a2, g2, k_k, k_a, r_k, lnx_w, lnx_b, tb=256):
    b, s, _ = p.shape
    dr = D_RWKV
    r, lw, k, v, kk, beta, g, bonus = _rw_prep(p, mu, w0, w2, a0, a2, g2, k_k, k_a, r_k)
    vec = jnp.concatenate([jnp.stack([lnx_w, lnx_b]), jnp.zeros((6, dr), F32)])
    avg = jnp.asarray(_head_ones() / HEAD_DIM, BF16)
    tile = pl.BlockSpec((1, tb, dr), lambda i, c: (i, c, 0))
    full = lambda arr: pl.BlockSpec(arr.shape, lambda i, c: (0, 0))
    return pl.pallas_call(
        functools.partial(_rw_scan_body, n_chunks=tb // RW_CHUNK),
        grid=(b, s // tb),
        in_specs=[tile] * 8 + [full(vec), full(avg)],
        out_specs=tile,
        out_shape=jax.ShapeDtypeStruct((b, s, dr), BF16),
        scratch_shapes=[pltpu.VMEM((RWKV_HEADS, HEAD_DIM, HEAD_DIM), F32)],
        compiler_params=_params(2), name="rwkv_scan")(r, lw, k, v, kk, beta, bonus, g, vec, avg)


def kernel(x, attn_norm, w_in, nsa_cmp_pos, nsa_cmp_w1, nsa_cmp_w2, nsa_out_gain, rw_mu, rw_w0, rw_w2, rw_a0, rw_a2, rw_g2, rw_k_k, rw_k_a, rw_r_k, rw_lnx_w, rw_lnx_b, moba_out_gain, w_out, ffn_norm, ffn_w_in, ffn_conv_w, ffn_conv_b, ffn_w_out, final_norm):
    b, s, d = x.shape
    depth = w_in.shape[0]
    d_ff = ffn_w_out.shape[1]
    g3 = NSA_GROUP * HEAD_DIM
    w_nsa = w_out[:, :D_NSA].reshape(depth, NSA_KV_HEADS, g3, d)
    w_nsa = jnp.pad(w_nsa, ((0, 0), (0, 0), (0, D_NSA_PAD - g3), (0, 0))).reshape(depth, NSA_KV_HEADS * D_NSA_PAD, d)
    w_nsa = w_nsa.astype(BF16)
    w_rw = w_out[:, D_NSA:D_NSA + D_RWKV].astype(BF16)
    w_moba = w_out[:, D_NSA + D_RWKV:].astype(BF16)
    ffn_w_in_b = ffn_w_in.astype(BF16)
    ffn_w_out_b = ffn_w_out.astype(BF16)
    for l in range(depth):
        wn, wt = _proj_weights(w_in[l])
        kv, ks, kw, rw, mk, qn, vs, vw, gt, mq, mv = _norm_proj(x, attn_norm[l], wn, wt)
        o_nsa = _nsa(qn, gt, kv, ks, vs, kw, vw, nsa_cmp_pos[l], nsa_cmp_w1[l], nsa_cmp_w2[l], nsa_out_gain[l])
        o_rw = _rwkv(rw, rw_mu[l], rw_w0[l], rw_w2[l], rw_a0[l], rw_a2[l], rw_g2[l], rw_k_k[l], rw_k_a[l],
                     rw_r_k[l], rw_lnx_w[l], rw_lnx_b[l])
        o_moba = _moba(mq, mk, mv, moba_out_gain[l])
        x = _out_proj(x.reshape(b * s, d), o_nsa.reshape(b * s, -1), o_rw.reshape(b * s, D_RWKV),
                      o_moba.reshape(b * s, D_MOBA), w_nsa[l], w_rw[l], w_moba[l]).reshape(b, s, d)
        x = _ffn(x, ffn_norm[l], ffn_w_in_b[l, :, :d_ff], ffn_w_in_b[l, :, d_ff:], ffn_conv_w[l], ffn_conv_b[l],
                 ffn_w_out_b[l], final_norm, final_norm=(l == depth - 1))
    return x
```

```python
import functools

import numpy as np
import jax
import jax.numpy as jnp
from jax import lax
from jax.experimental import pallas as pl
from jax.experimental.pallas import tpu as pltpu

HEAD_DIM = 64
NSA_HEADS = 6
NSA_KV_HEADS = 2
NSA_GROUP = NSA_HEADS // NSA_KV_HEADS
RWKV_HEADS = 4
MOBA_HEADS = 6
D_NSA = NSA_HEADS * HEAD_DIM
D_NSA_KV = NSA_KV_HEADS * HEAD_DIM
D_RWKV = RWKV_HEADS * HEAD_DIM
D_MOBA = MOBA_HEADS * HEAD_DIM
CMP_LEN = 32
CMP_STRIDE = 16
SLC_BLOCK = 64
SLC_TOPK = 16
WINDOW = 512
N_BRANCH = 3
RW_DECAY_SCALE = 0.606531
RW_LN_EPS = 64e-5
RW_LORA = 128
MOBA_BLOCK = 256
MOBA_TOPK = 3
CONV_WIDTH = 3
NORM_EPS = 1e-6
BIG = 1e9

N_GATE = NSA_HEADS * N_BRANCH
GATE_ROWS = 16
N_IN_RWKV = 3 * D_RWKV + RW_LORA
D_NSA_PAD = 256

NEG = -1e30
SCALE = HEAD_DIM ** -0.5
SCALE_LOG2E = SCALE * float(np.log2(np.e))
V_AUG = HEAD_DIM + 16
NSA_TQ = 256
NSA_TK = 256
RW_CHUNK = 64
PROJ_TM = 512
VMEM_LIMIT = 56 * 1024 * 1024

F32 = jnp.float32
BF16 = jnp.bfloat16
HI = lax.Precision.HIGHEST

_NN = (((1,), (0,)), ((), ()))
_NT = (((1,), (1,)), ((), ()))


def _params(n_axes):
    return pltpu.CompilerParams(dimension_semantics=("arbitrary",) * n_axes, vmem_limit_bytes=VMEM_LIMIT)


def _mm(a, b, precision=None):
    return jnp.dot(a, b, preferred_element_type=F32, precision=precision)


def _rms(x, gain):
    return x * lax.rsqrt(jnp.mean(x * x, axis=-1, keepdims=True) + NORM_EPS) * gain


def _split_bf16(x):
    hi = x.astype(BF16)
    return hi, (x - hi.astype(F32)).astype(BF16)


def _mm3(a, b, dims=_NN):
    ah, al = a if isinstance(a, tuple) else _split_bf16(a)
    bh, bl = b if isinstance(b, tuple) else _split_bf16(b)
    dg = lambda x, y: lax.dot_general(x, y, dims, preferred_element_type=F32)
    return dg(ah, bh) + (dg(ah, bl) + dg(al, bh))


def _mm2(a, b):
    ah, al = _split_bf16(a)
    return _mm(ah, b) + _mm(al, b)


def _each(f, *lists):
    return [f(*xs) for xs in zip(*lists)]


_NAT = dict(kc=(0, 128), vc=(128, 256), ks=(256, 384), kw=(384, 512), rw=(512, 512 + N_IN_RWKV),
            mk=(512 + N_IN_RWKV, 512 + N_IN_RWKV + D_MOBA))
N_NAT = 512 + N_IN_RWKV + D_MOBA
_TR = dict(qn=(0, D_NSA), vs=(D_NSA, D_NSA + 128), vw=(D_NSA + 128, D_NSA + 256),
           gt=(D_NSA + 256, D_NSA + 256 + NSA_KV_HEADS * GATE_ROWS))
_TR['mq'] = (_TR['gt'][1], _TR['gt'][1] + D_MOBA)
_TR['mv'] = (_TR['mq'][1], _TR['mq'][1] + D_MOBA)
N_TR = _TR['mv'][1]


def _norm_proj_body(x_ref, g_ref, wn_ref, wt_ref, kv_ref, ks_ref, kw_ref, rw_ref, mk_ref, qn_ref, vs_ref, vw_ref,
                    gt_ref, mq_ref, mv_ref):
    h = _rms(x_ref[0], g_ref[...]).astype(BF16)
    tm = h.shape[0]
    nat = _mm(h, wn_ref[...])
    tr = lax.dot_general(wt_ref[...], h, _NT, preferred_element_type=F32)
    cut = lambda name: nat[:, _NAT[name][0]:_NAT[name][1]]
    rows = lambda name: tr[_TR[name][0]:_TR[name][1], :]
    kv_ref[0, 0] = cut('kc')
    kv_ref[1, 0] = cut('vc')
    ks_ref[0] = cut('ks').astype(BF16).reshape(tm // NSA_TK, NSA_TK, D_NSA_KV)
    kw_ref[0] = cut('kw').astype(BF16).reshape(tm // NSA_TQ, NSA_TQ, D_NSA_KV)
    rw_ref[0] = cut('rw')
    mk_ref[0] = cut('mk').astype(BF16).reshape(tm // MOBA_BLOCK, MOBA_BLOCK, D_MOBA)
    qn_ref[0] = rows('qn').astype(BF16)
    gt_ref[0] = rows('gt')
    ones = jnp.ones((V_AUG - HEAD_DIM, tm), F32)

    def with_ones(t):
        parts = []
        for h in range(t.shape[0] // HEAD_DIM):
            parts += [t[h * HEAD_DIM:(h + 1) * HEAD_DIM], ones]
        return jnp.concatenate(parts, axis=0)

    for t, ref, width in ((with_ones(rows('vs')), vs_ref, NSA_TK), (with_ones(rows('vw')), vw_ref, NSA_TQ),
                          (rows('mq'), mq_ref, MOBA_BLOCK), (with_ones(rows('mv')), mv_ref, MOBA_BLOCK)):
        t = t.astype(BF16)
        for i in range(tm // width):
            ref[0, i] = t[:, i * width:(i + 1) * width]


def _proj_weights(w_in):
    kvw = D_NSA_KV
    o = D_NSA
    q_nsa = w_in[:, :o]
    kc, vc, ks, vs, kw, vw = (w_in[:, o + i * kvw:o + (i + 1) * kvw] for i in range(6))
    o += 6 * kvw
    gates = w_in[:, o:o + N_GATE]
    o += N_GATE
    rw = w_in[:, o:o + N_IN_RWKV]
    o += N_IN_RWKV
    mq, mk, mv = (w_in[:, o + i * D_MOBA:o + (i + 1) * D_MOBA] for i in range(3))
    src = np.zeros((NSA_KV_HEADS * GATE_ROWS,), np.int32)
    used = np.zeros((NSA_KV_HEADS * GATE_ROWS,), np.float32)
    for g in range(NSA_KV_HEADS):
        for r in range(NSA_GROUP):
            for br in range(N_BRANCH):
                src[g * GATE_ROWS + br * NSA_GROUP + r] = (g * NSA_GROUP + r) * N_BRANCH + br
                used[g * GATE_ROWS + br * NSA_GROUP + r] = 1.0
    gates_t = gates[:, src] * used[None, :]
    wn = jnp.concatenate([kc, vc, ks, kw, rw, mk], axis=1).astype(BF16)
    wt = jnp.concatenate([q_nsa, vs, vw, gates_t, mq, mv], axis=1).T.astype(BF16)
    return wn, wt


def _norm_proj(x, gain, wn, wt, tm=PROJ_TM):
    b, s, d = x.shape
    tok = lambda w, dt: (jax.ShapeDtypeStruct((b, s, w), dt), pl.BlockSpec((1, tm, w), lambda i, j: (i, j, 0)))
    tiles = lambda t, shape, dt: (jax.ShapeDtypeStruct((b, s // t) + shape, dt),
                                  pl.BlockSpec((1, tm // t) + shape, lambda i, j: (i, j, 0, 0)))
    lanes = lambda r, dt: (jax.ShapeDtypeStruct((b, r, s), dt), pl.BlockSpec((1, r, tm), lambda i, j: (i, 0, j)))
    outs = [
        (jax.ShapeDtypeStruct((2, b, s, D_NSA_KV), F32), pl.BlockSpec((2, 1, tm, D_NSA_KV), lambda i, j: (0, i, j, 0))),
        tiles(NSA_TK, (NSA_TK, D_NSA_KV), BF16),
        tiles(NSA_TQ, (NSA_TQ, D_NSA_KV), BF16),
        tok(N_IN_RWKV, F32),
        tiles(MOBA_BLOCK, (MOBA_BLOCK, D_MOBA), BF16),
        lanes(D_NSA, BF16),
        tiles(NSA_TK, (NSA_KV_HEADS * V_AUG, NSA_TK), BF16),
        tiles(NSA_TQ, (NSA_KV_HEADS * V_AUG, NSA_TQ), BF16),
        lanes(NSA_KV_HEADS * GATE_ROWS, F32),
        tiles(MOBA_BLOCK, (D_MOBA, MOBA_BLOCK), BF16),
        tiles(MOBA_BLOCK, (MOBA_HEADS * V_AUG, MOBA_BLOCK), BF16),
    ]
    full = lambda arr: pl.BlockSpec(arr.shape, lambda i, j: (0,) * arr.ndim)
    gain = gain.reshape(1, d)
    return pl.pallas_call(
        _norm_proj_body,
        grid=(b, s // tm),
        in_specs=[pl.BlockSpec((1, tm, d), lambda i, j: (i, j, 0)), full(gain), full(wn), full(wt)],
        out_specs=[o[1] for o in outs],
        out_shape=[o[0] for o in outs],
        compiler_params=_params(2), name="norm_proj")(x, gain, wn, wt)


def _out_proj_body(x_ref, a_ref, b_ref, c_ref, wa_ref, wb_ref, wc_ref, o_ref):
    o_ref[...] = (x_ref[...] + _mm(a_ref[...], wa_ref[...]) + _mm(b_ref[...], wb_ref[...])
                  + _mm(c_ref[...], wc_ref[...]))


def _out_proj(x2d, a, b, c, wa, wb, wc, tm=512):
    m, d = x2d.shape
    row = lambda w: pl.BlockSpec((tm, w), lambda i: (i, 0))
    full = lambda arr: pl.BlockSpec(arr.shape, lambda i: (0, 0))
    return pl.pallas_call(
        _out_proj_body,
        grid=(m // tm,),
        in_specs=[row(d), row(a.shape[1]), row(b.shape[1]), row(c.shape[1]), full(wa), full(wb), full(wc)],
        out_specs=row(d),
        out_shape=jax.ShapeDtypeStruct((m, d), F32),
        compiler_params=_params(1), name="out_proj")(x2d, a, b, c, wa, wb, wc)


def _ffn_body(x_ref, g_ref, wu_ref, wg_ref, cw_ref, cb_ref, wo_ref, fg_ref, o_ref, carry_ref, *, tf, final_norm):
    @pl.when(pl.program_id(1) == 0)
    def _():
        carry_ref[...] = jnp.zeros_like(carry_ref)

    x = x_ref[0]
    tm = x.shape[0]
    h = _rms(x, g_ref[...]).astype(BF16)
    row = lax.broadcasted_iota(jnp.int32, (tm, tf), 0)
    acc = x
    for c in range(wu_ref.shape[1] // tf):
        cs = slice(c * tf, (c + 1) * tf)
        u = _mm(h, wu_ref[:, cs])
        g = _mm(h, wg_ref[:, cs])
        prev = carry_ref[:, cs]
        g1 = jnp.where(row == 0, prev[7:8, :], pltpu.roll(g, 1, axis=0))
        g2 = jnp.where(row == 0, prev[6:7, :], jnp.where(row == 1, prev[7:8, :], pltpu.roll(g, 2, axis=0)))
        carry_ref[:, cs] = g[tm - 8:, :]
        cw = cw_ref[:, cs]
        gc = cw[0:1, :] * g2 + cw[1:2, :] * g1 + cw[2:3, :] * g + cb_ref[:, cs]
        act = (gc * jax.nn.sigmoid(gc) * u).astype(BF16)
        acc = acc + _mm(act, wo_ref[cs, :])
    if final_norm:
        acc = _rms(acc, fg_ref[...])
    o_ref[0] = acc


def _ffn(x, gain, wu, wg, cw, cb, wo, fgain, final_norm, tm=256, tf=1408):
    b, s, d = x.shape
    dff = wu.shape[1]
    full = lambda arr: pl.BlockSpec(arr.shape, lambda i, j: (0, 0))
    gain = gain.reshape(1, d)
    cb = cb.reshape(1, dff)
    fgain = fgain.reshape(1, d)
    return pl.pallas_call(
        functools.partial(_ffn_body, tf=tf, final_norm=final_norm),
        grid=(b, s // tm),
        in_specs=[pl.BlockSpec((1, tm, d), lambda i, j: (i, j, 0)), full(gain), full(wu), full(wg), full(cw),
                  full(cb), full(wo), full(fgain)],
        out_specs=pl.BlockSpec((1, tm, d), lambda i, j: (i, j, 0)),
        out_shape=jax.ShapeDtypeStruct((b, s, d), F32),
        scratch_shapes=[pltpu.VMEM((8, dff), F32)],
        compiler_params=_params(2), name="conv_glu")(x, gain, wu, wg, cw, cb, wo, fgain)


def _fold8(x, op):
    return op(x.reshape(x.shape[0] // 8, 8, x.shape[1]), axis=0)


def _probs(s, m):
    return jnp.exp2((s - m).astype(BF16))


UNROLLS = (8, 4, 2, 1)


def _tree(op, xs):
    while len(xs) > 1:
        xs = [op(xs[i], xs[i + 1]) if i + 1 < len(xs) else xs[i] for i in range(0, len(xs), 2)]
    return xs[0]


def _grouped_reduce(n, tile, op, init):
    start, carry = 0, init
    for width in UNROLLS:
        count = (n - start) // width

        def group(i, c, width=width, start=start):
            return op(c, _tree(op, [tile(start + i * width + u) for u in range(width)]))

        carry = lax.fori_loop(0, count, group, carry)
        start = start + count * width
    return carry


def _pad_heads(q, slot, n_slots):
    zero = jnp.zeros_like(q)
    if isinstance(slot, int):
        parts = [q if i == slot else zero for i in range(n_slots)]
    else:
        parts = [jnp.where(slot == i, q, zero) for i in range(n_slots)]
    return jnp.concatenate(parts, axis=0)


def _moba_body(q_ref, k_ref, vT_ref, gain_ref, o_ref, kmean_ref, bias_ref, s_ref, *, nb):
    qi = pl.program_id(2)
    blk, d = MOBA_BLOCK, HEAD_DIM
    pair = range(2)

    @pl.when(qi == 0)
    def _():
        kmean_ref[...] = jnp.mean(k_ref[0].astype(F32), axis=1)

    q2 = q_ref[0, 0]
    qp = [_pad_heads(q2[h * d:(h + 1) * d], h, 2) for h in pair]
    km_hi, km_lo = _split_bf16(kmean_ref[...])
    bid = lax.broadcasted_iota(jnp.int32, (nb, blk), 0)
    for h in pair:
        gate = _mm(km_hi, qp[h]) + _mm(km_lo, qp[h])
        gate = jnp.where(bid < qi, gate, -jnp.inf)
        sel = jnp.zeros((nb, blk), F32)
        for _ in range(min(MOBA_TOPK, max(nb - 1, 1))):
            m = jnp.max(gate, axis=0, keepdims=True)
            cand = (gate == m) & (m > -jnp.inf)
            idx = jnp.min(jnp.where(cand, bid, nb), axis=0, keepdims=True)
            hit = bid == idx
            sel = jnp.where(hit, 1.0, sel)
            gate = jnp.where(hit, -jnp.inf, gate)
        bias_ref[h] = jnp.where(sel > 0.5, 0.0, NEG)

    kpos = lax.broadcasted_iota(jnp.int32, (blk, blk), 0)
    qpos = lax.broadcasted_iota(jnp.int32, (blk, blk), 1)
    k_own = k_ref[0, qi]
    m8 = []
    for h in pair:
        s = jnp.where(kpos <= qpos, _mm(k_own, qp[h]) * SCALE_LOG2E, NEG)
        s_ref[h, qi] = s
        m8.append(_fold8(s, jnp.max))

    def score(j):
        kj = k_ref[0, j]
        out = []
        for h in pair:
            s = _mm(kj, qp[h]) * SCALE_LOG2E + bias_ref[h, pl.ds(j, 1), :]
            s_ref[h, j] = s
            out.append(_fold8(s, jnp.max))
        return tuple(out)

    both = lambda op: (lambda a, b: tuple(op(a[h], b[h]) for h in pair))
    m8 = _grouped_reduce(qi, score, both(jnp.maximum), tuple(m8))
    m = [jnp.max(x, axis=0, keepdims=True) for x in m8]

    def weighted(j):
        vj = vT_ref[0, j]
        return tuple(_mm(vj[h * V_AUG:(h + 1) * V_AUG], _probs(s_ref[h, j], m[h])) for h in pair)

    zero = jnp.zeros((V_AUG, blk), F32)
    res = _grouped_reduce(qi + 1, weighted, both(jnp.add), (zero, zero))
    outs = []
    for h in pair:
        o = res[h][:d] / jnp.maximum(res[h][d:d + 1], 1e-30)
        outs.append(o * lax.rsqrt(jnp.mean(o * o, axis=0, keepdims=True) + NORM_EPS) * gain_ref[0, h])
    o_ref[0] = jnp.concatenate(outs, axis=0).T.astype(o_ref.dtype)


def _moba(qT, kb, vT, gain):
    b, nb, dm, blk = qT.shape
    d2 = 2 * HEAD_DIM
    return pl.pallas_call(
        functools.partial(_moba_body, nb=nb),
        grid=(b, dm // d2, nb),
        in_specs=[pl.BlockSpec((1, 1, d2, blk), lambda i, p, c: (i, c, p, 0)),
                  pl.BlockSpec((1, nb, blk, d2), lambda i, p, c: (i, 0, 0, p)),
                  pl.BlockSpec((1, nb, 2 * V_AUG, blk), lambda i, p, c: (i, 0, p, 0)),
                  pl.BlockSpec((1, 2, HEAD_DIM, 1), lambda i, p, c: (p, 0, 0, 0))],
        out_specs=pl.BlockSpec((1, blk, d2), lambda i, p, c: (i, c, p)),
        out_shape=jax.ShapeDtypeStruct((b, nb * blk, dm), BF16),
        scratch_shapes=[pltpu.VMEM((nb, d2), F32), pltpu.VMEM((2, nb, blk), F32),
                        pltpu.VMEM((2, nb, blk, blk), F32)],
        compiler_params=_params(3), name="moba")(qT, kb, vT, gain.reshape(dm // d2, 2, HEAD_DIM, 1))


def _gelu_tanh(x):
    return x * (0.5 * (1.0 + jnp.tanh(np.sqrt(2.0 / np.pi) * (x + 0.044715 * (x * x * x)))))


def _nsa_cmp_body(r_ref, pos_ref, wtop_ref, wbot_ref, w2_ref, o_ref, oT_ref):
    r = r_ref[0, 0]
    nc = r.shape[0]
    y = _mm(r + pos_ref[0, 0], wtop_ref[0], HI)
    z = _mm(r + pos_ref[0, 1], wbot_ref[0], HI)
    pre = y + pltpu.roll(z, nc - 1, axis=0)
    o = _mm(_gelu_tanh(pre), w2_ref[0], HI)
    o_ref[0, 0] = o
    oT_ref[0, 0] = o.T


def _nsa_compress(kv, pos, w1, w2):
    _, b, s, _ = kv.shape
    g, d = NSA_KV_HEADS, HEAD_DIM
    nc = s // CMP_STRIDE
    hid = w1.shape[-1]
    half = CMP_LEN // 2
    eye = jnp.eye(g, dtype=F32)
    w1r = w1.reshape(2, 2, half, d, hid)
    w1p = jnp.einsum('thjdc,gk->thjgdkc', w1r, eye).reshape(2, 2, half * g * d, g * hid)
    posp = jnp.broadcast_to(pos.reshape(2, 2, half, 1, d), (2, 2, half, g, d)).reshape(2, 2, 1, half * g * d)
    w2p = jnp.einsum('tcd,gk->tgckd', w2, eye).reshape(2, g * hid, g * d)
    kd = half * g * d
    r = kv.reshape(2, b, nc, kd)
    return pl.pallas_call(
        _nsa_cmp_body,
        grid=(2, b),
        in_specs=[pl.BlockSpec((1, 1, nc, kd), lambda t, i: (t, i, 0, 0)),
                  pl.BlockSpec((1, 2, 1, kd), lambda t, i: (t, 0, 0, 0)),
                  pl.BlockSpec((1, kd, g * hid), lambda t, i: (t, 0, 0)),
                  pl.BlockSpec((1, kd, g * hid), lambda t, i: (t, 0, 0)),
                  pl.BlockSpec((1, g * hid, g * d), lambda t, i: (t, 0, 0))],
        out_specs=[pl.BlockSpec((1, 1, nc, g * d), lambda t, i: (t, i, 0, 0)),
                   pl.BlockSpec((1, 1, g * d, nc), lambda t, i: (t, i, 0, 0))],
        out_shape=[jax.ShapeDtypeStruct((2, b, nc, g * d), F32), jax.ShapeDtypeStruct((2, b, g * d, nc), F32)],
        compiler_params=_params(2), name="nsa_compress")(r, posp, w1p[:, 0], w1p[:, 1], w2p)


def _cmp_to_slc_T(nc, n_slc):
    r = SLC_BLOCK // CMP_STRIDE
    c = CMP_LEN // CMP_STRIDE
    i = (r * np.arange(n_slc)[:, None, None] - np.arange(r)[None, :, None] - np.arange(c)[None, None, :]).reshape(n_slc, -1)
    m = (i[:, :, None] == np.arange(nc - 1)[None, None, :]).sum(1)
    return np.concatenate([m, np.zeros((n_slc, 1), m.dtype)], axis=1).astype(np.float32)


def _nsa_attn_body(q_ref, gt_ref, kc_ref, vcT_ref, map_ref, ks_ref, vsT_ref, kw_ref, vwT_ref, gain_ref, o_ref,
                   bias_ref, s_ref, *, nc, n_slc):
    grp = pl.program_id(1)
    c = pl.program_id(2)
    tq, rr, d = NSA_TQ, NSA_GROUP, HEAD_DIM
    nl = tq * rr
    t0 = c * tq
    q3 = q_ref[0]
    q = jnp.concatenate([q3[r * d:(r + 1) * d] for r in range(rr)], axis=1)
    qp = _pad_heads(q, grp, NSA_KV_HEADS)
    lane = lax.broadcasted_iota(jnp.int32, (1, nl), 1)
    tpos3 = t0 + (lane & (tq - 1))

    sc = _mm2(kc_ref[0, 0], qp) * SCALE
    n_id = lax.broadcasted_iota(jnp.int32, (nc, nl), 0)
    cmask = (n_id * CMP_STRIDE + (CMP_LEN - 1)) <= tpos3
    scm = jnp.where(cmask, sc, NEG)
    pc = jnp.where(cmask, jnp.exp(scm - jnp.max(scm, axis=0, keepdims=True)), 0.0)
    pc = pc / jnp.maximum(jnp.sum(pc, axis=0, keepdims=True), 1e-30)
    oc = _mm(vcT_ref[0, 0].astype(BF16), pc.astype(BF16))

    pcs = pc[:, 0:tq]
    for r in range(1, rr):
        pcs = pcs + pc[:, r * tq:(r + 1) * tq]
    p_hi, p_lo = _split_bf16(pcs)
    imp = _mm(map_ref[...], p_hi) + _mm(map_ref[...], p_lo)
    bid = lax.broadcasted_iota(jnp.int32, (n_slc, tq), 0)
    tpos = t0 + lax.broadcasted_iota(jnp.int32, (1, tq), 1)
    cur = tpos // SLC_BLOCK
    forced = (bid == 0) | (bid == cur) | (bid == cur - 1)
    val = jnp.where(forced, BIG, jnp.where(bid * SLC_BLOCK <= tpos, imp, -BIG))
    sel = jnp.zeros((n_slc, tq), F32)
    for _ in range(min(SLC_TOPK, n_slc)):
        m = jnp.max(val, axis=0, keepdims=True)
        idx = jnp.min(jnp.where(val == m, bid, n_slc), axis=0, keepdims=True)
        hit = bid == idx
        sel = jnp.where(hit, 1.0, sel)
        val = jnp.where(hit, -jnp.inf, val)
    bias_ref[...] = jnp.where(sel > 0.5, 0.0, NEG)

    per_tile = NSA_TK // SLC_BLOCK

    def sel_scores(j, diagonal):
        s = _mm(ks_ref[0, j], qp) * SCALE_LOG2E
        parts = []
        for i in range(per_tile):
            brow = bias_ref[pl.ds(per_tile * j + i, 1), :]
            parts.append(s[i * SLC_BLOCK:(i + 1) * SLC_BLOCK, :] + jnp.concatenate([brow] * rr, axis=1))
        s = jnp.concatenate(parts, axis=0)
        if diagonal:
            kpos = j * NSA_TK + lax.broadcasted_iota(jnp.int32, (NSA_TK, nl), 0)
            s = jnp.where(kpos <= tpos3, s, NEG)
        s_ref[j] = s
        return _fold8(s, jnp.max)

    jl = t0 // NSA_TK
    m8 = _grouped_reduce(jl, lambda j: sel_scores(j, False), jnp.maximum, sel_scores(jl, True))
    m_s = jnp.max(m8, axis=0, keepdims=True)

    acc_s = _grouped_reduce(jl + 1, lambda j: _mm(vsT_ref[0, j], _probs(s_ref[j], m_s)), jnp.add,
                            jnp.zeros((V_AUG, nl), F32))
    o_s = acc_s[:d] / jnp.maximum(acc_s[d:d + 1], 1e-30)

    n_w = WINDOW // tq
    w_tiles = []
    for i in range(n_w + 1):
        widx = c - n_w + i
        wcl = jnp.maximum(widx, 0)
        s = _mm(kw_ref[0, wcl], qp) * SCALE_LOG2E
        kpos = widx * tq + lax.broadcasted_iota(jnp.int32, (tq, nl), 0)
        dist = tpos3 - kpos
        w_tiles.append((jnp.where((dist >= 0) & (dist < WINDOW) & (kpos >= 0), s, NEG), wcl))
    m8 = functools.reduce(jnp.maximum, [_fold8(s, jnp.max) for s, _ in w_tiles])
    m_w = jnp.max(m8, axis=0, keepdims=True)
    acc_w = _tree(jnp.add, [_mm(vwT_ref[0, wcl], _probs(s, m_w)) for s, wcl in w_tiles])
    o_w = acc_w[:d] / jnp.maximum(acc_w[d:d + 1], 1e-30)

    g = jax.nn.sigmoid(gt_ref[0])
    outs = []
    for r in range(rr):
        sl = slice(r * tq, (r + 1) * tq)
        o = g[r:r + 1, :] * oc[:, sl] + g[rr + r:rr + r + 1, :] * o_s[:, sl] + g[2 * rr + r:2 * rr + r + 1, :] * o_w[:, sl]
        o = o * lax.rsqrt(jnp.mean(o * o, axis=0, keepdims=True) + NORM_EPS) * gain_ref[0, r]
        outs.append(o)
    outs.append(jnp.zeros((D_NSA_PAD - rr * d, tq), F32))
    o_ref[0] = jnp.concatenate(outs, axis=0).T.astype(o_ref.dtype)


def _nsa(qn, gt, kv, ks, vsT, kw, vwT, cmp_pos, cmp_w1, cmp_w2, gain):
    b, _, s = qn.shape
    g, rr, d, tq = NSA_KV_HEADS, NSA_GROUP, HEAD_DIM, NSA_TQ
    nq, nl = s // tq, NSA_TQ * NSA_GROUP
    nc, n_slc = s // CMP_STRIDE, s // SLC_BLOCK
    cmp, cmpT = _nsa_compress(kv, cmp_pos, cmp_w1, cmp_w2)
    cmap = jnp.asarray(_cmp_to_slc_T(nc, n_slc), BF16)
    whole = lambda arr: pl.BlockSpec((1,) + arr.shape[1:], lambda i, j, c: (i, 0, 0, 0))
    return pl.pallas_call(
        functools.partial(_nsa_attn_body, nc=nc, n_slc=n_slc),
        grid=(b, g, nq),
        in_specs=[pl.BlockSpec((1, rr * d, tq), lambda i, j, c: (i, j, c)),
                  pl.BlockSpec((1, GATE_ROWS, tq), lambda i, j, c: (i, j, c)),
                  pl.BlockSpec((1, 1, nc, g * d), lambda i, j, c: (0, i, 0, 0)),
                  pl.BlockSpec((1, 1, d, nc), lambda i, j, c: (1, i, j, 0)),
                  pl.BlockSpec((n_slc, nc), lambda i, j, c: (0, 0)),
                  whole(ks),
                  pl.BlockSpec((1, s // NSA_TK, V_AUG, NSA_TK), lambda i, j, c: (i, 0, j, 0)),
                  whole(kw),
                  pl.BlockSpec((1, nq, V_AUG, tq), lambda i, j, c: (i, 0, j, 0)),
                  pl.BlockSpec((1, rr, d, 1), lambda i, j, c: (j, 0, 0, 0))],
        out_specs=pl.BlockSpec((1, tq, D_NSA_PAD), lambda i, j, c: (i, c, j)),
        out_shape=jax.ShapeDtypeStruct((b, s, g * D_NSA_PAD), BF16),
        scratch_shapes=[pltpu.VMEM((n_slc, tq), F32), pltpu.VMEM((s // NSA_TK, NSA_TK, nl), F32)],
        compiler_params=_params(3), name="nsa_attn")(qn, gt, cmp, cmpT, cmap, ks, vsT, kw, vwT,
                                                      gain.reshape(g, rr, d, 1))


def _head_ones():
    i = np.arange(D_RWKV) // HEAD_DIM
    return (i[:, None] == i[None, :]).astype(np.float32)


def _rw_prep_body(p_ref, mu_ref, w2_ref, a2_ref, g2_ref, vec_ref, ones_ref,
                  r_ref, lw_ref, k_ref, v_ref, kk_ref, b_ref, g_ref, bonus_ref, carry_ref):
    @pl.when(pl.program_id(1) == 0)
    def _():
        carry_ref[...] = jnp.zeros_like(carry_ref)

    p = p_ref[0]
    t = p.shape[0]
    row = lax.broadcasted_iota(jnp.int32, p.shape, 0)
    prev = jnp.where(row == 0, carry_ref[7:8, :], pltpu.roll(p, 1, axis=0))
    carry_ref[...] = p[t - 8:, :]
    xs = p + (prev - p) * mu_ref[...]
    dr = D_RWKV
    r, k, v, lora = xs[:, :dr], xs[:, dr:2 * dr], xs[:, 2 * dr:3 * dr], xs[:, 3 * dr:]
    w0, a0, k_k, k_a, r_k = (vec_ref[i:i + 1, :] for i in range(5))
    ones = ones_ref[...]
    logw = -RW_DECAY_SCALE * jax.nn.sigmoid(w0 + _mm(jnp.tanh(lora), w2_ref[...], HI))
    a = jax.nn.sigmoid(a0 + _mm(lora, a2_ref[...], HI))
    g_ref[0] = _mm(jax.nn.sigmoid(lora), g2_ref[...], HI)
    kk = k * k_k
    kk = kk / jnp.maximum(jnp.sqrt(_mm(kk * kk, ones, HI)), 1e-12)
    k = k * (1.0 + (a - 1.0) * k_a)
    r_ref[0] = r
    lw_ref[0] = logw
    k_ref[0] = k
    v_ref[0] = v
    kk_ref[0] = kk
    b_ref[0] = kk * a
    bonus_ref[0] = _mm(r * k * r_k, ones, HI) * v


def _rw_prep(p, mu, w0, w2, a0, a2, g2, k_k, k_a, r_k, tm=512):
    b, s, n = p.shape
    dr = D_RWKV
    nl = n - 3 * dr
    pad = lambda w, lo: jnp.zeros((nl, dr), F32).at[lo:lo + w.shape[0]].set(w)
    w2p, a2p, g2p = pad(w2, 0), pad(a2, w2.shape[0]), pad(g2, w2.shape[0] + a2.shape[0])
    vec = jnp.concatenate([jnp.stack([w0, a0, k_k, k_a, r_k.reshape(dr)]), jnp.zeros((3, dr), F32)])
    full = lambda arr: pl.BlockSpec(arr.shape, lambda i, j: (0, 0))
    ones = jnp.asarray(_head_ones())
    mu = mu.reshape(1, n)
    tile = pl.BlockSpec((1, tm, dr), lambda i, j: (i, j, 0))
    return pl.pallas_call(
        _rw_prep_body,
        grid=(b, s // tm),
        in_specs=[pl.BlockSpec((1, tm, n), lambda i, j: (i, j, 0)), full(mu), full(w2p), full(a2p), full(g2p),
                  full(vec), full(ones)],
        out_specs=[tile] * 8,
        out_shape=[jax.ShapeDtypeStruct((b, s, dr), F32)] * 8,
        scratch_shapes=[pltpu.VMEM((8, n), F32)],
        compiler_params=_params(2), name="rwkv_prep")(p, mu, w2p, a2p, g2p, vec, ones)


def _cumsum_rows(x):
    n = x.shape[0]
    row = lax.broadcasted_iota(jnp.int32, x.shape, 0)
    d = 1
    while d < n:
        x = x + jnp.where(row >= d, pltpu.roll(x, d, axis=0), 0.0)
        d *= 2
    return x


def _unit_lower_inverses(ns, row, col):
    eye = (row == col).astype(F32)
    size = ns[0].shape[0]
    n8 = [jnp.where((row >> 3) == (col >> 3), n, 0.0) for n in ns]
    n8s = _each(_split_bf16, n8)
    n8_2 = _each(lambda a: _mm3(a, a), n8s)
    n8_2s = _each(_split_bf16, n8_2)
    n8_4 = _each(lambda a: _mm3(a, a), n8_2s)
    p1 = _each(lambda n, n2, a, a2: eye + n + n2 + _mm3(a, a2), n8, n8_2, n8s, n8_2s)
    t = _each(lambda p, n4: p + _mm3(p, n4), p1, n8_4)
    sh = 4
    while (1 << (sh - 1)) < size:
        off = ((row >> sh) == (col >> sh)) & ((row >> (sh - 1)) != (col >> (sh - 1)))
        ts = _each(_split_bf16, t)
        tc = _each(lambda a, n: _mm3(a, jnp.where(off, n, 0.0)), ts, ns)
        t = _each(lambda x, y, a: x + _mm3(y, a), t, tc, ts)
        sh += 1
    return t


def _rw_scan_body(r_ref, lw_ref, k_ref, v_ref, kk_ref, b_ref, bonus_ref, g_ref, vec_ref, avg_ref, o_ref, h_ref,
                  *, n_chunks):
    @pl.when(pl.program_id(1) == 0)
    def _():
        h_ref[...] = jnp.zeros_like(h_ref)

    cs, d, nh = RW_CHUNK, HEAD_DIM, RWKV_HEADS
    row = lax.broadcasted_iota(jnp.int32, (cs, cs), 0)
    col = lax.broadcasted_iota(jnp.int32, (cs, cs), 1)
    eye = (row == col).astype(F32)
    cat0 = lambda *xs: jnp.concatenate(xs, axis=0)
    cat1 = lambda *xs: jnp.concatenate(xs, axis=1)
    units = [(c, h) for c in range(n_chunks) for h in range(nh)]
    take = lambda ref: [ref[0, c * cs:(c + 1) * cs, :][:, h * d:(h + 1) * d] for c, h in units]
    r, lw, k, v, kk, beta = (take(ref) for ref in (r_ref, lw_ref, k_ref, v_ref, kk_ref, b_ref))

    cum = _each(_cumsum_rows, lw)
    tot = [x[cs - 1:cs, :] for x in cum]
    a_t = _each(lambda kk_, c_, l_: -kk_ * jnp.exp(c_ - l_), kk, cum, lw)
    r_t = _each(lambda r_, c_: r_ * jnp.exp(c_), r, cum)
    b_t = _each(lambda b_, c_: b_ * jnp.exp(-c_), beta, cum)
    k_t = _each(lambda k_, c_: k_ * jnp.exp(-c_), k, cum)
    b_hT = _each(lambda b_, t_, c_: (b_ * jnp.exp(t_ - c_)).T, beta, tot, cum)
    k_hT = _each(lambda k_, t_, c_: (k_ * jnp.exp(t_ - c_)).T, k, tot, cum)
    vs = _each(_split_bf16, v)
    gram = _each(lambda a_, r_, b_, k_: _mm3(cat0(a_, r_), cat0(b_, k_), _NT), a_t, r_t, b_t, k_t)
    a_ab = [jnp.where(row > col, x[:cs, :cs], 0.0) for x in gram]
    a_ak = [jnp.where(row > col, x[:cs, cs:], 0.0) for x in gram]
    m_rb = [jnp.where(row >= col, x[cs:, :cs], 0.0) for x in gram]
    m_rk = [jnp.where(row >= col, x[cs:, cs:], 0.0) for x in gram]
    t_inv = _unit_lower_inverses(a_ab, row, col)
    akv = _each(_mm3, a_ak, vs)
    rkv = _each(_mm3, m_rk, vs)
    khv = _each(_mm3, k_hT, vs)
    wus = _each(lambda t_, a_, x_: _split_bf16(_mm3(t_, cat1(a_, x_))), t_inv, a_t, akv)
    qo = _each(lambda m_, w_, r_, x_: _mm3(m_, w_) + cat1(r_, x_), m_rb, wus, r_t, rkv)
    pd = _each(lambda b_, w_, t_, x_: _mm3(b_, w_) + cat1(eye * jnp.exp(t_), x_), b_hT, wus, tot, khv)

    hs = [h_ref[h] for h in range(nh)]
    outs = []
    for c in range(n_chunks):
        heads = []
        for h in range(nh):
            u = c * nh + h
            qp = _mm3(cat0(qo[u][:, :cs], pd[u][:, :cs]), hs[h])
            heads.append(qp[:cs] + qo[u][:, cs:])
            hs[h] = qp[cs:] + pd[u][:, cs:]
        outs.append(cat1(*heads))
    for h in range(nh):
        h_ref[h] = hs[h]

    o = cat0(*outs)
    avg = avg_ref[...]
    mean = _mm2(o, avg)
    ctr = o - mean
    var = _mm2(ctr * ctr, avg)
    y = ctr * lax.rsqrt(var + RW_LN_EPS) * vec_ref[0:1, :] + vec_ref[1:2, :]
    o_ref[0] = ((y + bonus_ref[0]) * g_ref[0]).astype(o_ref.dtype)


def _rwkv(p, mu, w0, w2, a0, a2, g2, k_k, k_a, r_k, lnx_w, lnx_b, tb=256):
    b, s, _ = p.shape
    dr = D_RWKV
    r, lw, k, v, kk, beta, g, bonus = _rw_prep(p, mu, w0, w2, a0, a2, g2, k_k, k_a, r_k)
    vec = jnp.concatenate([jnp.stack([lnx_w, lnx_b]), jnp.zeros((6, dr), F32)])
    avg = jnp.asarray(_head_ones() / HEAD_DIM, BF16)
    tile = pl.BlockSpec((1, tb, dr), lambda i, c: (i, c, 0))
    full = lambda arr: pl.BlockSpec(arr.shape, lambda i, c: (0, 0))
    return pl.pallas_call(
        functools.partial(_rw_scan_body, n_chunks=tb // RW_CHUNK),
        grid=(b, s // tb),
        in_specs=[tile] * 8 + [full(vec), full(avg)],
        out_specs=tile,
        out_shape=jax.ShapeDtypeStruct((b, s, dr), BF16),
        scratch_shapes=[pltpu.VMEM((RWKV_HEADS, HEAD_DIM, HEAD_DIM), F32)],
        compiler_params=_params(2), name="rwkv_scan")(r, lw, k, v, kk, beta, bonus, g, vec, avg)


def kernel(x, attn_norm, w_in, nsa_cmp_pos, nsa_cmp_w1, nsa_cmp_w2, nsa_out_gain, rw_mu, rw_w0, rw_w2, rw_a0, rw_a2, rw_g2, rw_k_k, rw_k_a, rw_r_k, rw_lnx_w, rw_lnx_b, moba_out_gain, w_out, ffn_norm, ffn_w_in, ffn_conv_w, ffn_conv_b, ffn_w_out, final_norm):
    b, s, d = x.shape
    depth = w_in.shape[0]
    d_ff = ffn_w_out.shape[1]
    g3 = NSA_GROUP * HEAD_DIM
    w_nsa = w_out[:, :D_NSA].reshape(depth, NSA_KV_HEADS, g3, d)
    w_nsa = jnp.pad(w_nsa, ((0, 0), (0, 0), (0, D_NSA_PAD - g3), (0, 0))).reshape(depth, NSA_KV_HEADS * D_NSA_PAD, d)
    w_nsa = w_nsa.astype(BF16)
    w_rw = w_out[:, D_NSA:D_NSA + D_RWKV].astype(BF16)
    w_moba = w_out[:, D_NSA + D_RWKV:].astype(BF16)
    ffn_w_in_b = ffn_w_in.astype(BF16)
    ffn_w_out_b = ffn_w_out.astype(BF16)
    for l in range(depth):
        wn, wt = _proj_weights(w_in[l])
        kv, ks, kw, rw, mk, qn, vs, vw, gt, mq, mv = _norm_proj(x, attn_norm[l], wn, wt)
        o_nsa = _nsa(qn, gt, kv, ks, vs, kw, vw, nsa_cmp_pos[l], nsa_cmp_w1[l], nsa_cmp_w2[l], nsa_out_gain[l])
        o_rw = _rwkv(rw, rw_mu[l], rw_w0[l], rw_w2[l], rw_a0[l], rw_a2[l], rw_g2[l], rw_k_k[l], rw_k_a[l],
                     rw_r_k[l], rw_lnx_w[l], rw_lnx_b[l])
        o_moba = _moba(mq, mk, mv, moba_out_gain[l])
        x = _out_proj(x.reshape(b * s, d), o_nsa.reshape(b * s, -1), o_rw.reshape(b * s, D_RWKV),
                      o_moba.reshape(b * s, D_MOBA), w_nsa[l], w_rw[l], w_moba[l]).reshape(b, s, d)
        x = _ffn(x, ffn_norm[l], ffn_w_in_b[l, :, :d_ff], ffn_w_in_b[l, :, d_ff:], ffn_conv_w[l], ffn_conv_b[l],
                 ffn_w_out_b[l], final_norm, final_norm=(l == depth - 1))
    return x
```

```python
import functools

import numpy as np
import jax
import jax.numpy as jnp
from jax import lax
from jax.experimental import pallas as pl
from jax.experimental.pallas import tpu as pltpu

HEAD_DIM = 64
NSA_HEADS = 6
NSA_KV_HEADS = 2
NSA_GROUP = NSA_HEADS // NSA_KV_HEADS
RWKV_HEADS = 4
MOBA_HEADS = 6
D_NSA = NSA_HEADS * HEAD_DIM
D_NSA_KV = NSA_KV_HEADS * HEAD_DIM
D_RWKV = RWKV_HEADS * HEAD_DIM
D_MOBA = MOBA_HEADS * HEAD_DIM
CMP_LEN = 32
CMP_STRIDE = 16
SLC_BLOCK = 64
SLC_TOPK = 16
WINDOW = 512
N_BRANCH = 3
RW_DECAY_SCALE = 0.606531
RW_LN_EPS = 64e-5
RW_LORA = 128
MOBA_BLOCK = 256
MOBA_TOPK = 3
CONV_WIDTH = 3
NORM_EPS = 1e-6
BIG = 1e9

N_GATE = NSA_HEADS * N_BRANCH
GATE_ROWS = 16
N_IN_RWKV = 3 * D_RWKV + RW_LORA
D_NSA_PAD = 256

NEG = -1e30
SCALE = HEAD_DIM ** -0.5
SCALE_LOG2E = SCALE * float(np.log2(np.e))
V_AUG = HEAD_DIM + 16
NSA_TQ = 256
NSA_TK = 256
assert WINDOW % NSA_TQ == 0 and WINDOW >= NSA_TQ and NSA_TK % SLC_BLOCK == 0
RW_CHUNK = 64
PROJ_TM = 512
VMEM_LIMIT = 56 * 1024 * 1024

F32 = jnp.float32
BF16 = jnp.bfloat16
HI = lax.Precision.HIGHEST

_NN = (((1,), (0,)), ((), ()))
_NT = (((1,), (1,)), ((), ()))


def _params(n_axes):
    return pltpu.CompilerParams(dimension_semantics=("arbitrary",) * n_axes, vmem_limit_bytes=VMEM_LIMIT)


def _mm(a, b, precision=None):
    return jnp.dot(a, b, preferred_element_type=F32, precision=precision)


def _rms(x, gain):
    return x * lax.rsqrt(jnp.mean(x * x, axis=-1, keepdims=True) + NORM_EPS) * gain


def _split_bf16(x):
    hi = x.astype(BF16)
    return hi, (x - hi.astype(F32)).astype(BF16)


def _mm3(a, b, dims=_NN):
    ah, al = a if isinstance(a, tuple) else _split_bf16(a)
    bh, bl = b if isinstance(b, tuple) else _split_bf16(b)
    dg = lambda x, y: lax.dot_general(x, y, dims, preferred_element_type=F32)
    return dg(ah, bh) + (dg(ah, bl) + dg(al, bh))


def _mm2(a, b):
    ah, al = _split_bf16(a)
    return _mm(ah, b) + _mm(al, b)


def _each(f, *lists):
    return [f(*xs) for xs in zip(*lists)]


_NAT = dict(kc=(0, 128), vc=(128, 256), ks=(256, 384), kw=(384, 512), rw=(512, 512 + N_IN_RWKV),
            mk=(512 + N_IN_RWKV, 512 + N_IN_RWKV + D_MOBA))
N_NAT = 512 + N_IN_RWKV + D_MOBA
_TR = dict(qn=(0, D_NSA), vs=(D_NSA, D_NSA + 128), vw=(D_NSA + 128, D_NSA + 256),
           gt=(D_NSA + 256, D_NSA + 256 + NSA_KV_HEADS * GATE_ROWS))
_TR['mq'] = (_TR['gt'][1], _TR['gt'][1] + D_MOBA)
_TR['mv'] = (_TR['mq'][1], _TR['mq'][1] + D_MOBA)
N_TR = _TR['mv'][1]


def _norm_proj_body(x_ref, g_ref, wn_ref, wt_ref, kv_ref, ks_ref, kw_ref, rw_ref, mk_ref, qn_ref, vs_ref, vw_ref,
                    gt_ref, mq_ref, mv_ref):
    h = _rms(x_ref[0], g_ref[...]).astype(BF16)
    tm = h.shape[0]
    nat = _mm(h, wn_ref[...])
    tr = lax.dot_general(wt_ref[...], h, _NT, preferred_element_type=F32)
    cut = lambda name: nat[:, _NAT[name][0]:_NAT[name][1]]
    rows = lambda name: tr[_TR[name][0]:_TR[name][1], :]
    kv_ref[0, 0] = cut('kc')
    kv_ref[1, 0] = cut('vc')
    ks_ref[0] = cut('ks').astype(BF16).reshape(tm // NSA_TK, NSA_TK, D_NSA_KV)
    kw_ref[0] = cut('kw').astype(BF16).reshape(tm // NSA_TQ, NSA_TQ, D_NSA_KV)
    rw_ref[0] = cut('rw')
    mk_ref[0] = cut('mk').astype(BF16).reshape(tm // MOBA_BLOCK, MOBA_BLOCK, D_MOBA)
    qn_ref[0] = rows('qn').astype(BF16)
    gt_ref[0] = rows('gt')
    ones = jnp.ones((V_AUG - HEAD_DIM, tm), F32)

    def with_ones(t):
        parts = []
        for h in range(t.shape[0] // HEAD_DIM):
            parts += [t[h * HEAD_DIM:(h + 1) * HEAD_DIM], ones]
        return jnp.concatenate(parts, axis=0)

    for t, ref, width in ((with_ones(rows('vs')), vs_ref, NSA_TK), (with_ones(rows('vw')), vw_ref, NSA_TQ),
                          (rows('mq'), mq_ref, MOBA_BLOCK), (with_ones(rows('mv')), mv_ref, MOBA_BLOCK)):
        t = t.astype(BF16)
        for i in range(tm // width):
            ref[0, i] = t[:, i * width:(i + 1) * width]


def _proj_weights(w_in):
    kvw = D_NSA_KV
    o = D_NSA
    q_nsa = w_in[:, :o]
    kc, vc, ks, vs, kw, vw = (w_in[:, o + i * kvw:o + (i + 1) * kvw] for i in range(6))
    o += 6 * kvw
    gates = w_in[:, o:o + N_GATE]
    o += N_GATE
    rw = w_in[:, o:o + N_IN_RWKV]
    o += N_IN_RWKV
    mq, mk, mv = (w_in[:, o + i * D_MOBA:o + (i + 1) * D_MOBA] for i in range(3))
    src = np.zeros((NSA_KV_HEADS * GATE_ROWS,), np.int32)
    used = np.zeros((NSA_KV_HEADS * GATE_ROWS,), np.float32)
    for g in range(NSA_KV_HEADS):
        for r in range(NSA_GROUP):
            for br in range(N_BRANCH):
                src[g * GATE_ROWS + br * NSA_GROUP + r] = (g * NSA_GROUP + r) * N_BRANCH + br
                used[g * GATE_ROWS + br * NSA_GROUP + r] = 1.0
    gates_t = gates[:, src] * used[None, :]
    wn = jnp.concatenate([kc, vc, ks, kw, rw, mk], axis=1).astype(BF16)
    wt = jnp.concatenate([q_nsa, vs, vw, gates_t, mq, mv], axis=1).T.astype(BF16)
    return wn, wt


def _norm_proj(x, gain, wn, wt, tm=PROJ_TM):
    b, s, d = x.shape
    tok = lambda w, dt: (jax.ShapeDtypeStruct((b, s, w), dt), pl.BlockSpec((1, tm, w), lambda i, j: (i, j, 0)))
    tiles = lambda t, shape, dt: (jax.ShapeDtypeStruct((b, s // t) + shape, dt),
                                  pl.BlockSpec((1, tm // t) + shape, lambda i, j: (i, j, 0, 0)))
    lanes = lambda r, dt: (jax.ShapeDtypeStruct((b, r, s), dt), pl.BlockSpec((1, r, tm), lambda i, j: (i, 0, j)))
    outs = [
        (jax.ShapeDtypeStruct((2, b, s, D_NSA_KV), F32), pl.BlockSpec((2, 1, tm, D_NSA_KV), lambda i, j: (0, i, j, 0))),
        tiles(NSA_TK, (NSA_TK, D_NSA_KV), BF16),
        tiles(NSA_TQ, (NSA_TQ, D_NSA_KV), BF16),
        tok(N_IN_RWKV, F32),
        tiles(MOBA_BLOCK, (MOBA_BLOCK, D_MOBA), BF16),
        lanes(D_NSA, BF16),
        tiles(NSA_TK, (NSA_KV_HEADS * V_AUG, NSA_TK), BF16),
        tiles(NSA_TQ, (NSA_KV_HEADS * V_AUG, NSA_TQ), BF16),
        lanes(NSA_KV_HEADS * GATE_ROWS, F32),
        tiles(MOBA_BLOCK, (D_MOBA, MOBA_BLOCK), BF16),
        tiles(MOBA_BLOCK, (MOBA_HEADS * V_AUG, MOBA_BLOCK), BF16),
    ]
    full = lambda arr: pl.BlockSpec(arr.shape, lambda i, j: (0,) * arr.ndim)
    gain = gain.reshape(1, d)
    return pl.pallas_call(
        _norm_proj_body,
        grid=(b, s // tm),
        in_specs=[pl.BlockSpec((1, tm, d), lambda i, j: (i, j, 0)), full(gain), full(wn), full(wt)],
        out_specs=[o[1] for o in outs],
        out_shape=[o[0] for o in outs],
        compiler_params=_params(2), name="norm_proj")(x, gain, wn, wt)


def _out_proj_body(x_ref, a_ref, b_ref, c_ref, wa_ref, wb_ref, wc_ref, o_ref):
    o_ref[...] = (x_ref[...] + _mm(a_ref[...], wa_ref[...]) + _mm(b_ref[...], wb_ref[...])
                  + _mm(c_ref[...], wc_ref[...]))


def _out_proj(x2d, a, b, c, wa, wb, wc, tm=512):
    m, d = x2d.shape
    row = lambda w: pl.BlockSpec((tm, w), lambda i: (i, 0))
    full = lambda arr: pl.BlockSpec(arr.shape, lambda i: (0, 0))
    return pl.pallas_call(
        _out_proj_body,
        grid=(m // tm,),
        in_specs=[row(d), row(a.shape[1]), row(b.shape[1]), row(c.shape[1]), full(wa), full(wb), full(wc)],
        out_specs=row(d),
        out_shape=jax.ShapeDtypeStruct((m, d), F32),
        compiler_params=_params(1), name="out_proj")(x2d, a, b, c, wa, wb, wc)


def _ffn_body(x_ref, g_ref, wu_ref, wg_ref, cw_ref, cb_ref, wo_ref, fg_ref, o_ref, carry_ref, *, tf, final_norm):
    @pl.when(pl.program_id(1) == 0)
    def _():
        carry_ref[...] = jnp.zeros_like(carry_ref)

    x = x_ref[0]
    tm = x.shape[0]
    h = _rms(x, g_ref[...]).astype(BF16)
    row = lax.broadcasted_iota(jnp.int32, (tm, tf), 0)
    acc = x
    for c in range(wu_ref.shape[1] // tf):
        cs = slice(c * tf, (c + 1) * tf)
        u = _mm(h, wu_ref[:, cs])
        g = _mm(h, wg_ref[:, cs])
        prev = carry_ref[:, cs]
        g1 = jnp.where(row == 0, prev[7:8, :], pltpu.roll(g, 1, axis=0))
        g2 = jnp.where(row == 0, prev[6:7, :], jnp.where(row == 1, prev[7:8, :], pltpu.roll(g, 2, axis=0)))
        carry_ref[:, cs] = g[tm - 8:, :]
        cw = cw_ref[:, cs]
        gc = cw[0:1, :] * g2 + cw[1:2, :] * g1 + cw[2:3, :] * g + cb_ref[:, cs]
        act = (gc * jax.nn.sigmoid(gc) * u).astype(BF16)
        acc = acc + _mm(act, wo_ref[cs, :])
    if final_norm:
        acc = _rms(acc, fg_ref[...])
    o_ref[0] = acc


def _ffn(x, gain, wu, wg, cw, cb, wo, fgain, final_norm, tm=256, tf=1408):
    b, s, d = x.shape
    dff = wu.shape[1]
    full = lambda arr: pl.BlockSpec(arr.shape, lambda i, j: (0, 0))
    gain = gain.reshape(1, d)
    cb = cb.reshape(1, dff)
    fgain = fgain.reshape(1, d)
    return pl.pallas_call(
        functools.partial(_ffn_body, tf=tf, final_norm=final_norm),
        grid=(b, s // tm),
        in_specs=[pl.BlockSpec((1, tm, d), lambda i, j: (i, j, 0)), full(gain), full(wu), full(wg), full(cw),
                  full(cb), full(wo), full(fgain)],
        out_specs=pl.BlockSpec((1, tm, d), lambda i, j: (i, j, 0)),
        out_shape=jax.ShapeDtypeStruct((b, s, d), F32),
        scratch_shapes=[pltpu.VMEM((8, dff), F32)],
        compiler_params=_params(2), name="conv_glu")(x, gain, wu, wg, cw, cb, wo, fgain)


def _fold8(x, op):
    return op(x.reshape(x.shape[0] // 8, 8, x.shape[1]), axis=0)


def _probs(s, m):
    return jnp.exp2((s - m).astype(BF16))


UNROLLS = (8, 4, 2, 1)


def _tree(op, xs):
    while len(xs) > 1:
        xs = [op(xs[i], xs[i + 1]) if i + 1 < len(xs) else xs[i] for i in range(0, len(xs), 2)]
    return xs[0]


def _grouped_reduce(n, tile, op, init):
    start, carry = 0, init
    for width in UNROLLS:
        count = (n - start) // width

        def group(i, c, width=width, start=start):
            return op(c, _tree(op, [tile(start + i * width + u) for u in range(width)]))

        carry = lax.fori_loop(0, count, group, carry)
        start = start + count * width
    return carry


def _pad_heads(q, slot, n_slots):
    zero = jnp.zeros_like(q)
    if isinstance(slot, int):
        parts = [q if i == slot else zero for i in range(n_slots)]
    else:
        parts = [jnp.where(slot == i, q, zero) for i in range(n_slots)]
    return jnp.concatenate(parts, axis=0)


def _moba_body(q_ref, k_ref, vT_ref, gain_ref, o_ref, kmean_ref, bias_ref, s_ref, *, nb):
    qi = pl.program_id(2)
    blk, d = MOBA_BLOCK, HEAD_DIM
    pair = range(2)

    @pl.when(qi == 0)
    def _():
        kmean_ref[...] = jnp.mean(k_ref[0].astype(F32), axis=1)

    q2 = q_ref[0, 0]
    qp = [_pad_heads(q2[h * d:(h + 1) * d], h, 2) for h in pair]
    km_hi, km_lo = _split_bf16(kmean_ref[...])
    bid = lax.broadcasted_iota(jnp.int32, (nb, blk), 0)
    for h in pair:
        gate = _mm(km_hi, qp[h]) + _mm(km_lo, qp[h])
        gate = jnp.where(bid < qi, gate, -jnp.inf)
        sel = jnp.zeros((nb, blk), F32)
        for _ in range(min(MOBA_TOPK, max(nb - 1, 1))):
            m = jnp.max(gate, axis=0, keepdims=True)
            cand = (gate == m) & (m > -jnp.inf)
            idx = jnp.min(jnp.where(cand, bid, nb), axis=0, keepdims=True)
            hit = bid == idx
            sel = jnp.where(hit, 1.0, sel)
            gate = jnp.where(hit, -jnp.inf, gate)
        bias_ref[h] = jnp.where(sel > 0.5, 0.0, NEG)

    kpos = lax.broadcasted_iota(jnp.int32, (blk, blk), 0)
    qpos = lax.broadcasted_iota(jnp.int32, (blk, blk), 1)
    k_own = k_ref[0, qi]
    m8 = []
    for h in pair:
        s = jnp.where(kpos <= qpos, _mm(k_own, qp[h]) * SCALE_LOG2E, NEG)
        s_ref[h, qi] = s
        m8.append(_fold8(s, jnp.max))

    def score(j):
        kj = k_ref[0, j]
        out = []
        for h in pair:
            s = _mm(kj, qp[h]) * SCALE_LOG2E + bias_ref[h, pl.ds(j, 1), :]
            s_ref[h, j] = s
            out.append(_fold8(s, jnp.max))
        return tuple(out)

    both = lambda op: (lambda a, b: tuple(op(a[h], b[h]) for h in pair))
    m8 = _grouped_reduce(qi, score, both(jnp.maximum), tuple(m8))
    m = [jnp.max(x, axis=0, keepdims=True) for x in m8]

    def weighted(j):
        vj = vT_ref[0, j]
        return tuple(_mm(vj[h * V_AUG:(h + 1) * V_AUG], _probs(s_ref[h, j], m[h])) for h in pair)

    zero = jnp.zeros((V_AUG, blk), F32)
    res = _grouped_reduce(qi + 1, weighted, both(jnp.add), (zero, zero))
    outs = []
    for h in pair:
        o = res[h][:d] / jnp.maximum(res[h][d:d + 1], 1e-30)
        outs.append(o * lax.rsqrt(jnp.mean(o * o, axis=0, keepdims=True) + NORM_EPS) * gain_ref[0, h])
    o_ref[0] = jnp.concatenate(outs, axis=0).T.astype(o_ref.dtype)


def _moba(qT, kb, vT, gain):
    b, nb, dm, blk = qT.shape
    d2 = 2 * HEAD_DIM
    return pl.pallas_call(
        functools.partial(_moba_body, nb=nb),
        grid=(b, dm // d2, nb),
        in_specs=[pl.BlockSpec((1, 1, d2, blk), lambda i, p, c: (i, c, p, 0)),
                  pl.BlockSpec((1, nb, blk, d2), lambda i, p, c: (i, 0, 0, p)),
                  pl.BlockSpec((1, nb, 2 * V_AUG, blk), lambda i, p, c: (i, 0, p, 0)),
                  pl.BlockSpec((1, 2, HEAD_DIM, 1), lambda i, p, c: (p, 0, 0, 0))],
        out_specs=pl.BlockSpec((1, blk, d2), lambda i, p, c: (i, c, p)),
        out_shape=jax.ShapeDtypeStruct((b, nb * blk, dm), BF16),
        scratch_shapes=[pltpu.VMEM((nb, d2), F32), pltpu.VMEM((2, nb, blk), F32),
                        pltpu.VMEM((2, nb, blk, blk), F32)],
        compiler_params=_params(3), name="moba")(qT, kb, vT, gain.reshape(dm // d2, 2, HEAD_DIM, 1))


def _gelu_tanh(x):
    return x * (0.5 * (1.0 + jnp.tanh(np.sqrt(2.0 / np.pi) * (x + 0.044715 * (x * x * x)))))


def _nsa_cmp_body(r_ref, pos_ref, wtop_ref, wbot_ref, w2_ref, o_ref, oT_ref):
    r = r_ref[0, 0]
    nc = r.shape[0]
    pair = lambda ref: (ref[0, 0], ref[0, 1])
    y = _mm3(r + pos_ref[0, 0], pair(wtop_ref))
    z = _mm3(r + pos_ref[0, 1], pair(wbot_ref))
    pre = y + pltpu.roll(z, nc - 1, axis=0)
    o = _mm3(_gelu_tanh(pre), pair(w2_ref))
    o_ref[0, 0] = o
    oT_ref[0, 0] = o.T


def _nsa_compress(kv, pos, w1, w2):
    _, b, s, _ = kv.shape
    g, d = NSA_KV_HEADS, HEAD_DIM
    nc = s // CMP_STRIDE
    hid = w1.shape[-1]
    half = CMP_LEN // 2
    eye = jnp.eye(g, dtype=F32)
    w1r = w1.reshape(2, 2, half, d, hid)
    w1p = jnp.einsum('thjdc,gk->thjgdkc', w1r, eye).reshape(2, 2, half * g * d, g * hid)
    posp = jnp.broadcast_to(pos.reshape(2, 2, half, 1, d), (2, 2, half, g, d)).reshape(2, 2, 1, half * g * d)
    w2p = jnp.einsum('tcd,gk->tgckd', w2, eye).reshape(2, g * hid, g * d)
    kd = half * g * d
    r = kv.reshape(2, b, nc, kd)
    hilo = lambda w: jnp.stack(_split_bf16(w), axis=1)
    wspec = lambda w: pl.BlockSpec((1,) + w.shape[1:], lambda t, i: (t, 0, 0, 0))
    wtop, wbot, w2p = hilo(w1p[:, 0]), hilo(w1p[:, 1]), hilo(w2p)
    return pl.pallas_call(
        _nsa_cmp_body,
        grid=(2, b),
        in_specs=[pl.BlockSpec((1, 1, nc, kd), lambda t, i: (t, i, 0, 0)),
                  pl.BlockSpec((1, 2, 1, kd), lambda t, i: (t, 0, 0, 0)),
                  wspec(wtop), wspec(wbot), wspec(w2p)],
        out_specs=[pl.BlockSpec((1, 1, nc, g * d), lambda t, i: (t, i, 0, 0)),
                   pl.BlockSpec((1, 1, g * d, nc), lambda t, i: (t, i, 0, 0))],
        out_shape=[jax.ShapeDtypeStruct((2, b, nc, g * d), F32), jax.ShapeDtypeStruct((2, b, g * d, nc), F32)],
        compiler_params=_params(2), name="nsa_compress")(r, posp, wtop, wbot, w2p)


def _cmp_to_slc_T(nc, n_slc):
    r = SLC_BLOCK // CMP_STRIDE
    c = CMP_LEN // CMP_STRIDE
    i = (r * np.arange(n_slc)[:, None, None] - np.arange(r)[None, :, None] - np.arange(c)[None, None, :]).reshape(n_slc, -1)
    m = (i[:, :, None] == np.arange(nc - 1)[None, None, :]).sum(1)
    return np.concatenate([m, np.zeros((n_slc, 1), m.dtype)], axis=1).astype(np.float32)


def _nsa_attn_body(q_ref, gt_ref, kc_ref, vcT_ref, map_ref, ks_ref, vsT_ref, kw_ref, vwT_ref, gain_ref, o_ref,
                   bias_ref, s_ref, *, nc, n_slc):
    grp = pl.program_id(1)
    c = pl.program_id(2)
    tq, rr, d = NSA_TQ, NSA_GROUP, HEAD_DIM
    nl = tq * rr
    t0 = c * tq
    q3 = q_ref[0]
    q = jnp.concatenate([q3[r * d:(r + 1) * d] for r in range(rr)], axis=1)
    qp = _pad_heads(q, grp, NSA_KV_HEADS)
    lane = lax.broadcasted_iota(jnp.int32, (1, nl), 1)
    tpos3 = t0 + (lane & (tq - 1))

    sc = _mm2(kc_ref[0, 0], qp) * SCALE
    n_id = lax.broadcasted_iota(jnp.int32, (nc, nl), 0)
    cmask = (n_id * CMP_STRIDE + (CMP_LEN - 1)) <= tpos3
    scm = jnp.where(cmask, sc, NEG)
    pc = jnp.where(cmask, jnp.exp(scm - jnp.max(scm, axis=0, keepdims=True)), 0.0)
    pc = pc / jnp.maximum(jnp.sum(pc, axis=0, keepdims=True), 1e-30)
    oc = _mm(vcT_ref[0, 0].astype(BF16), pc.astype(BF16))

    pcs = pc[:, 0:tq]
    for r in range(1, rr):
        pcs = pcs + pc[:, r * tq:(r + 1) * tq]
    p_hi, p_lo = _split_bf16(pcs)
    imp = _mm(map_ref[...], p_hi) + _mm(map_ref[...], p_lo)
    bid = lax.broadcasted_iota(jnp.int32, (n_slc, tq), 0)
    tpos = t0 + lax.broadcasted_iota(jnp.int32, (1, tq), 1)
    cur = tpos // SLC_BLOCK
    forced = (bid == 0) | (bid == cur) | (bid == cur - 1)
    val = jnp.where(forced, BIG, jnp.where(bid * SLC_BLOCK <= tpos, imp, -BIG))
    lanes = 128
    halves = [val[:, i * lanes:(i + 1) * lanes] for i in range(tq // lanes)]
    bid_h = bid[:, :lanes]
    for _ in range(min(SLC_TOPK, n_slc)):
        for i, v in enumerate(halves):
            m = jnp.max(v, axis=0, keepdims=True)
            idx = jnp.min(jnp.where(v == m, bid_h, n_slc), axis=0, keepdims=True)
            halves[i] = jnp.where(bid_h == idx, -jnp.inf, v)
    bias_ref[...] = jnp.where(jnp.concatenate(halves, axis=1) == -jnp.inf, 0.0, NEG)

    per_tile = NSA_TK // SLC_BLOCK

    def sel_scores(j, diagonal):
        s = _mm(ks_ref[0, j], qp) * SCALE_LOG2E
        parts = []
        for i in range(per_tile):
            brow = bias_ref[pl.ds(per_tile * j + i, 1), :]
            parts.append(s[i * SLC_BLOCK:(i + 1) * SLC_BLOCK, :] + jnp.concatenate([brow] * rr, axis=1))
        s = jnp.concatenate(parts, axis=0)
        if diagonal:
            kpos = j * NSA_TK + lax.broadcasted_iota(jnp.int32, (NSA_TK, nl), 0)
            s = jnp.where(kpos <= tpos3, s, NEG)
        s_ref[j] = s
        return _fold8(s, jnp.max)

    jl = t0 // NSA_TK
    m8 = _grouped_reduce(jl, lambda j: sel_scores(j, False), jnp.maximum, sel_scores(jl, True))
    m_s = jnp.max(m8, axis=0, keepdims=True)

    acc_s = _grouped_reduce(jl + 1, lambda j: _mm(vsT_ref[0, j], _probs(s_ref[j], m_s)), jnp.add,
                            jnp.zeros((V_AUG, nl), F32))
    o_s = acc_s[:d] / jnp.maximum(acc_s[d:d + 1], 1e-30)

    n_w = WINDOW // tq
    krow = lax.broadcasted_iota(jnp.int32, (tq, nl), 0)
    qcol = lax.broadcasted_iota(jnp.int32, (tq, nl), 1) & (tq - 1)
    w_tiles = []
    for i in range(n_w + 1):
        widx = c - n_w + i
        wcl = jnp.maximum(widx, 0)
        s = _mm(kw_ref[0, wcl], qp) * SCALE_LOG2E
        if i == 0:
            s = jnp.where(krow > qcol, s, NEG)
        if i == n_w:
            s = jnp.where(krow <= qcol, s, NEG)
        else:
            s = s + jnp.where(widx >= 0, 0.0, NEG)
        w_tiles.append((s, wcl))
    m8 = functools.reduce(jnp.maximum, [_fold8(s, jnp.max) for s, _ in w_tiles])
    m_w = jnp.max(m8, axis=0, keepdims=True)
    acc_w = _tree(jnp.add, [_mm(vwT_ref[0, wcl], _probs(s, m_w)) for s, wcl in w_tiles])
    o_w = acc_w[:d] / jnp.maximum(acc_w[d:d + 1], 1e-30)

    g = jax.nn.sigmoid(gt_ref[0])
    outs = []
    for r in range(rr):
        sl = slice(r * tq, (r + 1) * tq)
        o = g[r:r + 1, :] * oc[:, sl] + g[rr + r:rr + r + 1, :] * o_s[:, sl] + g[2 * rr + r:2 * rr + r + 1, :] * o_w[:, sl]
        o = o * lax.rsqrt(jnp.mean(o * o, axis=0, keepdims=True) + NORM_EPS) * gain_ref[0, r]
        outs.append(o)
    outs.append(jnp.zeros((D_NSA_PAD - rr * d, tq), F32))
    o_ref[0] = jnp.concatenate(outs, axis=0).T.astype(o_ref.dtype)


def _nsa(qn, gt, kv, ks, vsT, kw, vwT, cmp_pos, cmp_w1, cmp_w2, gain):
    b, _, s = qn.shape
    g, rr, d, tq = NSA_KV_HEADS, NSA_GROUP, HEAD_DIM, NSA_TQ
    nq, nl = s // tq, NSA_TQ * NSA_GROUP
    nc, n_slc = s // CMP_STRIDE, s // SLC_BLOCK
    cmp, cmpT = _nsa_compress(kv, cmp_pos, cmp_w1, cmp_w2)
    cmap = jnp.asarray(_cmp_to_slc_T(nc, n_slc), BF16)
    whole = lambda arr: pl.BlockSpec((1,) + arr.shape[1:], lambda i, j, c: (i, 0, 0, 0))
    return pl.pallas_call(
        functools.partial(_nsa_attn_body, nc=nc, n_slc=n_slc),
        grid=(b, g, nq),
        in_specs=[pl.BlockSpec((1, rr * d, tq), lambda i, j, c: (i, j, c)),
                  pl.BlockSpec((1, GATE_ROWS, tq), lambda i, j, c: (i, j, c)),
                  pl.BlockSpec((1, 1, nc, g * d), lambda i, j, c: (0, i, 0, 0)),
                  pl.BlockSpec((1, 1, d, nc), lambda i, j, c: (1, i, j, 0)),
                  pl.BlockSpec((n_slc, nc), lambda i, j, c: (0, 0)),
                  whole(ks),
                  pl.BlockSpec((1, s // NSA_TK, V_AUG, NSA_TK), lambda i, j, c: (i, 0, j, 0)),
                  whole(kw),
                  pl.BlockSpec((1, nq, V_AUG, tq), lambda i, j, c: (i, 0, j, 0)),
                  pl.BlockSpec((1, rr, d, 1), lambda i, j, c: (j, 0, 0, 0))],
        out_specs=pl.BlockSpec((1, tq, D_NSA_PAD), lambda i, j, c: (i, c, j)),
        out_shape=jax.ShapeDtypeStruct((b, s, g * D_NSA_PAD), BF16),
        scratch_shapes=[pltpu.VMEM((n_slc, tq), F32), pltpu.VMEM((s // NSA_TK, NSA_TK, nl), F32)],
        compiler_params=_params(3), name="nsa_attn")(qn, gt, cmp, cmpT, cmap, ks, vsT, kw, vwT,
                                                      gain.reshape(g, rr, d, 1))


def _head_ones():
    i = np.arange(D_RWKV) // HEAD_DIM
    return (i[:, None] == i[None, :]).astype(np.float32)


def _rw_prep_body(p_ref, mu_ref, w2_ref, a2_ref, g2_ref, vec_ref, ones_ref,
                  r_ref, lw_ref, k_ref, v_ref, kk_ref, b_ref, g_ref, bonus_ref, carry_ref):
    @pl.when(pl.program_id(1) == 0)
    def _():
        carry_ref[...] = jnp.zeros_like(carry_ref)

    p = p_ref[0]
    t = p.shape[0]
    row = lax.broadcasted_iota(jnp.int32, p.shape, 0)
    prev = jnp.where(row == 0, carry_ref[7:8, :], pltpu.roll(p, 1, axis=0))
    carry_ref[...] = p[t - 8:, :]
    xs = p + (prev - p) * mu_ref[...]
    dr = D_RWKV
    r, k, v, lora = xs[:, :dr], xs[:, dr:2 * dr], xs[:, 2 * dr:3 * dr], xs[:, 3 * dr:]
    w0, a0, k_k, k_a, r_k = (vec_ref[i:i + 1, :] for i in range(5))
    ones = ones_ref[...]
    logw = -RW_DECAY_SCALE * jax.nn.sigmoid(w0 + _mm3(jnp.tanh(lora), w2_ref[...]))
    a = jax.nn.sigmoid(a0 + _mm3(lora, a2_ref[...]))
    g_ref[0] = _mm3(jax.nn.sigmoid(lora), g2_ref[...])
    kk = k * k_k
    kk = kk / jnp.maximum(jnp.sqrt(_mm2(kk * kk, ones)), 1e-12)
    k = k * (1.0 + (a - 1.0) * k_a)
    r_ref[0] = r
    lw_ref[0] = logw
    k_ref[0] = k
    v_ref[0] = v
    kk_ref[0] = kk
    b_ref[0] = kk * a
    bonus_ref[0] = _mm2(r * k * r_k, ones) * v


def _rw_prep(p, mu, w0, w2, a0, a2, g2, k_k, k_a, r_k, tm=512):
    b, s, n = p.shape
    dr = D_RWKV
    nl = n - 3 * dr
    pad = lambda w, lo: jnp.zeros((nl, dr), F32).at[lo:lo + w.shape[0]].set(w)
    w2p, a2p, g2p = pad(w2, 0), pad(a2, w2.shape[0]), pad(g2, w2.shape[0] + a2.shape[0])
    vec = jnp.concatenate([jnp.stack([w0, a0, k_k, k_a, r_k.reshape(dr)]), jnp.zeros((3, dr), F32)])
    full = lambda arr: pl.BlockSpec(arr.shape, lambda i, j: (0, 0))
    ones = jnp.asarray(_head_ones(), BF16)
    mu = mu.reshape(1, n)
    tile = pl.BlockSpec((1, tm, dr), lambda i, j: (i, j, 0))
    return pl.pallas_call(
        _rw_prep_body,
        grid=(b, s // tm),
        in_specs=[pl.BlockSpec((1, tm, n), lambda i, j: (i, j, 0)), full(mu), full(w2p), full(a2p), full(g2p),
                  full(vec), full(ones)],
        out_specs=[tile] * 8,
        out_shape=[jax.ShapeDtypeStruct((b, s, dr), F32)] * 8,
        scratch_shapes=[pltpu.VMEM((8, n), F32)],
        compiler_params=_params(2), name="rwkv_prep")(p, mu, w2p, a2p, g2p, vec, ones)


def _cumsum_rows(x):
    n = x.shape[0]
    row = lax.broadcasted_iota(jnp.int32, x.shape, 0)
    d = 1
    while d < n:
        x = x + jnp.where(row >= d, pltpu.roll(x, d, axis=0), 0.0)
        d *= 2
    return x


def _unit_lower_inverses(ns, row, col):
    eye = (row == col).astype(F32)
    size = ns[0].shape[0]
    n8 = [jnp.where((row >> 3) == (col >> 3), n, 0.0) for n in ns]
    n8s = _each(_split_bf16, n8)
    n8_2 = _each(lambda a: _mm3(a, a), n8s)
    n8_2s = _each(_split_bf16, n8_2)
    n8_4 = _each(lambda a: _mm3(a, a), n8_2s)
    p1 = _each(lambda n, n2, a, a2: eye + n + n2 + _mm3(a, a2), n8, n8_2, n8s, n8_2s)
    t = _each(lambda p, n4: p + _mm3(p, n4), p1, n8_4)
    sh = 4
    while (1 << (sh - 1)) < size:
        off = ((row >> sh) == (col >> sh)) & ((row >> (sh - 1)) != (col >> (sh - 1)))
        ts = _each(_split_bf16, t)
        tc = _each(lambda a, n: _mm3(a, jnp.where(off, n, 0.0)), ts, ns)
        t = _each(lambda x, y, a: x + _mm3(y, a), t, tc, ts)
        sh += 1
    return t


def _rw_scan_body(r_ref, lw_ref, k_ref, v_ref, kk_ref, b_ref, bonus_ref, g_ref, vec_ref, avg_ref, o_ref, h_ref,
                  *, n_chunks):
    @pl.when(pl.program_id(1) == 0)
    def _():
        h_ref[...] = jnp.zeros_like(h_ref)

    cs, d, nh = RW_CHUNK, HEAD_DIM, RWKV_HEADS
    row = lax.broadcasted_iota(jnp.int32, (cs, cs), 0)
    col = lax.broadcasted_iota(jnp.int32, (cs, cs), 1)
    eye = (row == col).astype(F32)
    cat0 = lambda *xs: jnp.concatenate(xs, axis=0)
    cat1 = lambda *xs: jnp.concatenate(xs, axis=1)
    units = [(c, h) for c in range(n_chunks) for h in range(nh)]
    take = lambda ref: [ref[0, c * cs:(c + 1) * cs, :][:, h * d:(h + 1) * d] for c, h in units]
    r, lw, k, v, kk, beta = (take(ref) for ref in (r_ref, lw_ref, k_ref, v_ref, kk_ref, b_ref))

    cum = _each(_cumsum_rows, lw)
    tot = [x[cs - 1:cs, :] for x in cum]
    a_t = _each(lambda kk_, c_, l_: -kk_ * jnp.exp(c_ - l_), kk, cum, lw)
    r_t = _each(lambda r_, c_: r_ * jnp.exp(c_), r, cum)
    b_t = _each(lambda b_, c_: b_ * jnp.exp(-c_), beta, cum)
    k_t = _each(lambda k_, c_: k_ * jnp.exp(-c_), k, cum)
    b_hT = _each(lambda b_, t_, c_: (b_ * jnp.exp(t_ - c_)).T, beta, tot, cum)
    k_hT = _each(lambda k_, t_, c_: (k_ * jnp.exp(t_ - c_)).T, k, tot, cum)
    vs = _each(_split_bf16, v)
    bks = _each(lambda b_, k_: _split_bf16(cat0(b_, k_)), b_t, k_t)
    gram_a = _each(lambda a_, x_: _mm3(a_, x_, _NT), a_t, bks)
    gram_r = _each(lambda r_, x_: lax.dot_general(r_.astype(BF16), x_[0], _NT, preferred_element_type=F32), r_t, bks)
    a_ab = [jnp.where(row > col, x[:, :cs], 0.0) for x in gram_a]
    a_ak = [jnp.where(row > col, x[:, cs:], 0.0) for x in gram_a]
    m_rb = [jnp.where(row >= col, x[:, :cs], 0.0).astype(BF16) for x in gram_r]
    m_rk = [jnp.where(row >= col, x[:, cs:], 0.0).astype(BF16) for x in gram_r]
    t_inv = _unit_lower_inverses(a_ab, row, col)
    akv = _each(_mm3, a_ak, vs)
    rkv = _each(lambda m_, v_: _mm(m_, v_[0]), m_rk, vs)
    khv = _each(_mm3, k_hT, vs)
    wus = _each(lambda t_, a_, x_: _split_bf16(_mm3(t_, cat1(a_, x_))), t_inv, a_t, akv)
    qo = _each(lambda m_, w_, r_, x_: _mm(m_, w_[0]) + cat1(r_, x_), m_rb, wus, r_t, rkv)
    pd = _each(lambda b_, w_, t_, x_: _mm3(b_, w_) + cat1(eye * jnp.exp(t_), x_), b_hT, wus, tot, khv)

    hs = [h_ref[h] for h in range(nh)]
    outs = []
    for c in range(n_chunks):
        heads = []
        for h in range(nh):
            u = c * nh + h
            hsplit = _split_bf16(hs[h])
            heads.append(_mm(qo[u][:, :cs].astype(BF16), hsplit[0]) + qo[u][:, cs:])
            hs[h] = _mm3(pd[u][:, :cs], hsplit) + pd[u][:, cs:]
        outs.append(cat1(*heads))
    for h in range(nh):
        h_ref[h] = hs[h]

    o = cat0(*outs)
    avg = avg_ref[...]
    mean = _mm2(o, avg)
    ctr = o - mean
    var = _mm2(ctr * ctr, avg)
    y = ctr * lax.rsqrt(var + RW_LN_EPS) * vec_ref[0:1, :] + vec_ref[1:2, :]
    o_ref[0] = ((y + bonus_ref[0]) * g_ref[0]).astype(o_ref.dtype)


def _rwkv(p, mu, w0, w2, a0, a2, g2, k_k, k_a, r_k, lnx_w, lnx_b, tb=256):
    b, s, _ = p.shape
    dr = D_RWKV
    r, lw, k, v, kk, beta, g, bonus = _rw_prep(p, mu, w0, w2, a0, a2, g2, k_k, k_a, r_k)
    vec = jnp.concatenate([jnp.stack([lnx_w, lnx_b]), jnp.zeros((6, dr), F32)])
    avg = jnp.asarray(_head_ones() / HEAD_DIM, BF16)
    tile = pl.BlockSpec((1, tb, dr), lambda i, c: (i, c, 0))
    full = lambda arr: pl.BlockSpec(arr.shape, lambda i, c: (0, 0))
    return pl.pallas_call(
        functools.partial(_rw_scan_body, n_chunks=tb // RW_CHUNK),
        grid=(b, s // tb),
        in_specs=[tile] * 8 + [full(vec), full(avg)],
        out_specs=tile,
        out_shape=jax.ShapeDtypeStruct((b, s, dr), BF16),
        scratch_shapes=[pltpu.VMEM((RWKV_HEADS, HEAD_DIM, HEAD_DIM), F32)],
        compiler_params=_params(2), name="rwkv_scan")(r, lw, k, v, kk, beta, bonus, g, vec, avg)


def kernel(x, attn_norm, w_in, nsa_cmp_pos, nsa_cmp_w1, nsa_cmp_w2, nsa_out_gain, rw_mu, rw_w0, rw_w2, rw_a0, rw_a2, rw_g2, rw_k_k, rw_k_a, rw_r_k, rw_lnx_w, rw_lnx_b, moba_out_gain, w_out, ffn_norm, ffn_w_in, ffn_conv_w, ffn_conv_b, ffn_w_out, final_norm):
    b, s, d = x.shape
    depth = w_in.shape[0]
    d_ff = ffn_w_out.shape[1]
    g3 = NSA_GROUP * HEAD_DIM
    w_nsa = w_out[:, :D_NSA].reshape(depth, NSA_KV_HEADS, g3, d)
    w_nsa = jnp.pad(w_nsa, ((0, 0), (0, 0), (0, D_NSA_PAD - g3), (0, 0))).reshape(depth, NSA_KV_HEADS * D_NSA_PAD, d)
    w_nsa = w_nsa.astype(BF16)
    w_rw = w_out[:, D_NSA:D_NSA + D_RWKV].astype(BF16)
    w_moba = w_out[:, D_NSA + D_RWKV:].astype(BF16)
    ffn_w_in_b = ffn_w_in.astype(BF16)
    ffn_w_out_b = ffn_w_out.astype(BF16)
    for l in range(depth):
        wn, wt = _proj_weights(w_in[l])
        kv, ks, kw, rw, mk, qn, vs, vw, gt, mq, mv = _norm_proj(x, attn_norm[l], wn, wt)
        o_nsa = _nsa(qn, gt, kv, ks, vs, kw, vw, nsa_cmp_pos[l], nsa_cmp_w1[l], nsa_cmp_w2[l], nsa_out_gain[l])
        o_rw = _rwkv(rw, rw_mu[l], rw_w0[l], rw_w2[l], rw_a0[l], rw_a2[l], rw_g2[l], rw_k_k[l], rw_k_a[l],
                     rw_r_k[l], rw_lnx_w[l], rw_lnx_b[l])
        o_moba = _moba(mq, mk, mv, moba_out_gain[l])
        x = _out_proj(x.reshape(b * s, d), o_nsa.reshape(b * s, -1), o_rw.reshape(b * s, D_RWKV),
                      o_moba.reshape(b * s, D_MOBA), w_nsa[l], w_rw[l], w_moba[l]).reshape(b, s, d)
        x = _ffn(x, ffn_norm[l], ffn_w_in_b[l, :, :d_ff], ffn_w_in_b[l, :, d_ff:], ffn_conv_w[l], ffn_conv_b[l],
                 ffn_w_out_b[l], final_norm, final_norm=(l == depth - 1))
    return x
```

```python
import functools

import numpy as np
import jax
import jax.numpy as jnp
from jax import lax
from jax.experimental import pallas as pl
from jax.experimental.pallas import tpu as pltpu

HEAD_DIM = 64
NSA_HEADS = 6
NSA_KV_HEADS = 2
NSA_GROUP = NSA_HEADS // NSA_KV_HEADS
RWKV_HEADS = 4
MOBA_HEADS = 6
D_NSA = NSA_HEADS * HEAD_DIM
D_NSA_KV = NSA_KV_HEADS * HEAD_DIM
D_RWKV = RWKV_HEADS * HEAD_DIM
D_MOBA = MOBA_HEADS * HEAD_DIM
CMP_LEN = 32
CMP_STRIDE = 16
SLC_BLOCK = 64
SLC_TOPK = 16
WINDOW = 512
N_BRANCH = 3
RW_DECAY_SCALE = 0.606531
RW_LN_EPS = 64e-5
RW_LORA = 128
MOBA_BLOCK = 256
MOBA_TOPK = 3
CONV_WIDTH = 3
NORM_EPS = 1e-6
BIG = 1e9

N_GATE = NSA_HEADS * N_BRANCH
GATE_ROWS = 16
N_IN_RWKV = 3 * D_RWKV + RW_LORA
D_NSA_PAD = 256

NEG = -1e30
SCALE = HEAD_DIM ** -0.5
SCALE_LOG2E = SCALE * float(np.log2(np.e))
V_AUG = HEAD_DIM + 16
NSA_TQ = 256
NSA_TK = 256
assert WINDOW % NSA_TQ == 0 and WINDOW >= NSA_TQ and NSA_TK % SLC_BLOCK == 0
RW_CHUNK = 64
PROJ_TM = 512
VMEM_LIMIT = 56 * 1024 * 1024

F32 = jnp.float32
BF16 = jnp.bfloat16
HI = lax.Precision.HIGHEST

_NN = (((1,), (0,)), ((), ()))
_NT = (((1,), (1,)), ((), ()))


def _params(n_axes):
    return pltpu.CompilerParams(dimension_semantics=("arbitrary",) * n_axes, vmem_limit_bytes=VMEM_LIMIT)


def _mm(a, b, precision=None):
    return jnp.dot(a, b, preferred_element_type=F32, precision=precision)


def _rms(x, gain):
    return x * lax.rsqrt(jnp.mean(x * x, axis=-1, keepdims=True) + NORM_EPS) * gain


def _split_bf16(x):
    hi = x.astype(BF16)
    return hi, (x - hi.astype(F32)).astype(BF16)


def _mm3(a, b, dims=_NN):
    ah, al = a if isinstance(a, tuple) else _split_bf16(a)
    bh, bl = b if isinstance(b, tuple) else _split_bf16(b)
    dg = lambda x, y: lax.dot_general(x, y, dims, preferred_element_type=F32)
    return dg(ah, bh) + (dg(ah, bl) + dg(al, bh))


def _mm2(a, b):
    ah, al = _split_bf16(a)
    return _mm(ah, b) + _mm(al, b)


def _each(f, *lists):
    return [f(*xs) for xs in zip(*lists)]


_NAT = dict(kc=(0, 128), vc=(128, 256), ks=(256, 384), kw=(384, 512), rw=(512, 512 + N_IN_RWKV),
            mk=(512 + N_IN_RWKV, 512 + N_IN_RWKV + D_MOBA))
N_NAT = 512 + N_IN_RWKV + D_MOBA
_TR = dict(qn=(0, D_NSA), vs=(D_NSA, D_NSA + 128), vw=(D_NSA + 128, D_NSA + 256),
           gt=(D_NSA + 256, D_NSA + 256 + NSA_KV_HEADS * GATE_ROWS))
_TR['mq'] = (_TR['gt'][1], _TR['gt'][1] + D_MOBA)
_TR['mv'] = (_TR['mq'][1], _TR['mq'][1] + D_MOBA)
N_TR = _TR['mv'][1]


def _norm_proj_body(x_ref, g_ref, wn_ref, wt_ref, kv_ref, ks_ref, kw_ref, rw_ref, mk_ref, qn_ref, vs_ref, vw_ref,
                    gt_ref, mq_ref, mv_ref):
    h = _rms(x_ref[0], g_ref[...]).astype(BF16)
    tm = h.shape[0]
    nat = _mm(h, wn_ref[...])
    tr = lax.dot_general(wt_ref[...], h, _NT, preferred_element_type=F32)
    cut = lambda name: nat[:, _NAT[name][0]:_NAT[name][1]]
    rows = lambda name: tr[_TR[name][0]:_TR[name][1], :]
    kv_ref[0, 0] = cut('kc')
    kv_ref[1, 0] = cut('vc')
    ks_ref[0] = cut('ks').astype(BF16).reshape(tm // NSA_TK, NSA_TK, D_NSA_KV)
    kw_ref[0] = cut('kw').astype(BF16).reshape(tm // NSA_TQ, NSA_TQ, D_NSA_KV)
    rw_ref[0] = cut('rw')
    mk_ref[0] = cut('mk').astype(BF16).reshape(tm // MOBA_BLOCK, MOBA_BLOCK, D_MOBA)
    qn_ref[0] = rows('qn').astype(BF16)
    gt_ref[0] = rows('gt')
    ones = jnp.ones((V_AUG - HEAD_DIM, tm), F32)

    def with_ones(t):
        parts = []
        for h in range(t.shape[0] // HEAD_DIM):
            parts += [t[h * HEAD_DIM:(h + 1) * HEAD_DIM], ones]
        return jnp.concatenate(parts, axis=0)

    for t, ref, width in ((with_ones(rows('vs')), vs_ref, NSA_TK), (with_ones(rows('vw')), vw_ref, NSA_TQ),
                          (rows('mq'), mq_ref, MOBA_BLOCK), (with_ones(rows('mv')), mv_ref, MOBA_BLOCK)):
        t = t.astype(BF16)
        for i in range(tm // width):
            ref[0, i] = t[:, i * width:(i + 1) * width]


def _proj_weights(w_in):
    kvw = D_NSA_KV
    o = D_NSA
    q_nsa = w_in[:, :o]
    kc, vc, ks, vs, kw, vw = (w_in[:, o + i * kvw:o + (i + 1) * kvw] for i in range(6))
    o += 6 * kvw
    gates = w_in[:, o:o + N_GATE]
    o += N_GATE
    rw = w_in[:, o:o + N_IN_RWKV]
    o += N_IN_RWKV
    mq, mk, mv = (w_in[:, o + i * D_MOBA:o + (i + 1) * D_MOBA] for i in range(3))
    src = np.zeros((NSA_KV_HEADS * GATE_ROWS,), np.int32)
    used = np.zeros((NSA_KV_HEADS * GATE_ROWS,), np.float32)
    for g in range(NSA_KV_HEADS):
        for r in range(NSA_GROUP):
            for br in range(N_BRANCH):
                src[g * GATE_ROWS + br * NSA_GROUP + r] = (g * NSA_GROUP + r) * N_BRANCH + br
                used[g * GATE_ROWS + br * NSA_GROUP + r] = 1.0
    gates_t = gates[:, src] * used[None, :]
    wn = jnp.concatenate([kc, vc, ks, kw, rw, mk], axis=1).astype(BF16)
    wt = jnp.concatenate([q_nsa, vs, vw, gates_t, mq, mv], axis=1).T.astype(BF16)
    return wn, wt


def _norm_proj(x, gain, wn, wt, tm=PROJ_TM):
    b, s, d = x.shape
    tok = lambda w, dt: (jax.ShapeDtypeStruct((b, s, w), dt), pl.BlockSpec((1, tm, w), lambda i, j: (i, j, 0)))
    tiles = lambda t, shape, dt: (jax.ShapeDtypeStruct((b, s // t) + shape, dt),
                                  pl.BlockSpec((1, tm // t) + shape, lambda i, j: (i, j, 0, 0)))
    lanes = lambda r, dt: (jax.ShapeDtypeStruct((b, r, s), dt), pl.BlockSpec((1, r, tm), lambda i, j: (i, 0, j)))
    outs = [
        (jax.ShapeDtypeStruct((2, b, s, D_NSA_KV), F32), pl.BlockSpec((2, 1, tm, D_NSA_KV), lambda i, j: (0, i, j, 0))),
        tiles(NSA_TK, (NSA_TK, D_NSA_KV), BF16),
        tiles(NSA_TQ, (NSA_TQ, D_NSA_KV), BF16),
        tok(N_IN_RWKV, F32),
        tiles(MOBA_BLOCK, (MOBA_BLOCK, D_MOBA), BF16),
        lanes(D_NSA, BF16),
        tiles(NSA_TK, (NSA_KV_HEADS * V_AUG, NSA_TK), BF16),
        tiles(NSA_TQ, (NSA_KV_HEADS * V_AUG, NSA_TQ), BF16),
        lanes(NSA_KV_HEADS * GATE_ROWS, F32),
        tiles(MOBA_BLOCK, (D_MOBA, MOBA_BLOCK), BF16),
        tiles(MOBA_BLOCK, (MOBA_HEADS * V_AUG, MOBA_BLOCK), BF16),
    ]
    full = lambda arr: pl.BlockSpec(arr.shape, lambda i, j: (0,) * arr.ndim)
    gain = gain.reshape(1, d)
    return pl.pallas_call(
        _norm_proj_body,
        grid=(b, s // tm),
        in_specs=[pl.BlockSpec((1, tm, d), lambda i, j: (i, j, 0)), full(gain), full(wn), full(wt)],
        out_specs=[o[1] for o in outs],
        out_shape=[o[0] for o in outs],
        compiler_params=_params(2), name="norm_proj")(x, gain, wn, wt)


def _out_proj_body(x_ref, a_ref, b_ref, c_ref, wa_ref, wb_ref, wc_ref, o_ref):
    o_ref[...] = (x_ref[...] + _mm(a_ref[...], wa_ref[...]) + _mm(b_ref[...], wb_ref[...])
                  + _mm(c_ref[...], wc_ref[...]))


def _out_proj(x2d, a, b, c, wa, wb, wc, tm=512):
    m, d = x2d.shape
    row = lambda w: pl.BlockSpec((tm, w), lambda i: (i, 0))
    full = lambda arr: pl.BlockSpec(arr.shape, lambda i: (0, 0))
    return pl.pallas_call(
        _out_proj_body,
        grid=(m // tm,),
        in_specs=[row(d), row(a.shape[1]), row(b.shape[1]), row(c.shape[1]), full(wa), full(wb), full(wc)],
        out_specs=row(d),
        out_shape=jax.ShapeDtypeStruct((m, d), F32),
        compiler_params=_params(1), name="out_proj")(x2d, a, b, c, wa, wb, wc)


def _ffn_body(x_ref, g_ref, wu_ref, wg_ref, cw_ref, cb_ref, wo_ref, fg_ref, o_ref, carry_ref, *, tf, final_norm):
    @pl.when(pl.program_id(1) == 0)
    def _():
        carry_ref[...] = jnp.zeros_like(carry_ref)

    x = x_ref[0]
    tm = x.shape[0]
    h = _rms(x, g_ref[...]).astype(BF16)
    row = lax.broadcasted_iota(jnp.int32, (tm, tf), 0)
    acc = x
    for c in range(wu_ref.shape[1] // tf):
        cs = slice(c * tf, (c + 1) * tf)
        u = _mm(h, wu_ref[:, cs])
        g = _mm(h, wg_ref[:, cs])
        prev = carry_ref[:, cs]
        g1 = jnp.where(row == 0, prev[7:8, :], pltpu.roll(g, 1, axis=0))
        g2 = jnp.where(row == 0, prev[6:7, :], jnp.where(row == 1, prev[7:8, :], pltpu.roll(g, 2, axis=0)))
        carry_ref[:, cs] = g[tm - 8:, :]
        cw = cw_ref[:, cs]
        gc = cw[0:1, :] * g2 + cw[1:2, :] * g1 + cw[2:3, :] * g + cb_ref[:, cs]
        act = (gc * jax.nn.sigmoid(gc) * u).astype(BF16)
        acc = acc + _mm(act, wo_ref[cs, :])
    if final_norm:
        acc = _rms(acc, fg_ref[...])
    o_ref[0] = acc


def _ffn(x, gain, wu, wg, cw, cb, wo, fgain, final_norm, tm=512, tf=1408):
    b, s, d = x.shape
    dff = wu.shape[1]
    full = lambda arr: pl.BlockSpec(arr.shape, lambda i, j: (0, 0))
    gain = gain.reshape(1, d)
    cb = cb.reshape(1, dff)
    fgain = fgain.reshape(1, d)
    return pl.pallas_call(
        functools.partial(_ffn_body, tf=tf, final_norm=final_norm),
        grid=(b, s // tm),
        in_specs=[pl.BlockSpec((1, tm, d), lambda i, j: (i, j, 0)), full(gain), full(wu), full(wg), full(cw),
                  full(cb), full(wo), full(fgain)],
        out_specs=pl.BlockSpec((1, tm, d), lambda i, j: (i, j, 0)),
        out_shape=jax.ShapeDtypeStruct((b, s, d), F32),
        scratch_shapes=[pltpu.VMEM((8, dff), F32)],
        compiler_params=_params(2), name="conv_glu")(x, gain, wu, wg, cw, cb, wo, fgain)


def _fold8(x, op):
    return op(x.reshape(x.shape[0] // 8, 8, x.shape[1]), axis=0)


def _probs(s, m):
    return jnp.exp2((s - m).astype(BF16))


UNROLLS = (8, 4, 2, 1)


def _tree(op, xs):
    while len(xs) > 1:
        xs = [op(xs[i], xs[i + 1]) if i + 1 < len(xs) else xs[i] for i in range(0, len(xs), 2)]
    return xs[0]


def _grouped_reduce(n, tile, op, init):
    start, carry = 0, init
    for width in UNROLLS:
        count = (n - start) // width

        def group(i, c, width=width, start=start):
            return op(c, _tree(op, [tile(start + i * width + u) for u in range(width)]))

        carry = lax.fori_loop(0, count, group, carry)
        start = start + count * width
    return carry


def _pad_heads(q, slot, n_slots):
    zero = jnp.zeros_like(q)
    if isinstance(slot, int):
        parts = [q if i == slot else zero for i in range(n_slots)]
    else:
        parts = [jnp.where(slot == i, q, zero) for i in range(n_slots)]
    return jnp.concatenate(parts, axis=0)


def _moba_body(q_ref, k_ref, vT_ref, gain_ref, o_ref, kmean_ref, bias_ref, s_ref, *, nb):
    qi = pl.program_id(2)
    blk, d = MOBA_BLOCK, HEAD_DIM
    pair = range(2)

    @pl.when(qi == 0)
    def _():
        kmean_ref[...] = jnp.mean(k_ref[0].astype(F32), axis=1)

    q2 = q_ref[0, 0]
    qp = [_pad_heads(q2[h * d:(h + 1) * d], h, 2) for h in pair]
    km_hi, km_lo = _split_bf16(kmean_ref[...])
    bid = lax.broadcasted_iota(jnp.int32, (nb, blk), 0)
    for h in pair:
        gate = _mm(km_hi, qp[h]) + _mm(km_lo, qp[h])
        gate = jnp.where(bid < qi, gate, -jnp.inf)
        sel = jnp.zeros((nb, blk), F32)
        for _ in range(min(MOBA_TOPK, max(nb - 1, 1))):
            m = jnp.max(gate, axis=0, keepdims=True)
            cand = (gate == m) & (m > -jnp.inf)
            idx = jnp.min(jnp.where(cand, bid, nb), axis=0, keepdims=True)
            hit = bid == idx
            sel = jnp.where(hit, 1.0, sel)
            gate = jnp.where(hit, -jnp.inf, gate)
        bias_ref[h] = jnp.where(sel > 0.5, 0.0, NEG)

    kpos = lax.broadcasted_iota(jnp.int32, (blk, blk), 0)
    qpos = lax.broadcasted_iota(jnp.int32, (blk, blk), 1)
    k_own = k_ref[0, qi]
    m8 = []
    for h in pair:
        s = jnp.where(kpos <= qpos, _mm(k_own, qp[h]) * SCALE_LOG2E, NEG)
        s_ref[h, qi] = s
        m8.append(_fold8(s, jnp.max))

    def score(j):
        kj = k_ref[0, j]
        out = []
        for h in pair:
            s = _mm(kj, qp[h]) * SCALE_LOG2E + bias_ref[h, pl.ds(j, 1), :]
            s_ref[h, j] = s
            out.append(_fold8(s, jnp.max))
        return tuple(out)

    both = lambda op: (lambda a, b: tuple(op(a[h], b[h]) for h in pair))
    m8 = _grouped_reduce(qi, score, both(jnp.maximum), tuple(m8))
    m = [jnp.max(x, axis=0, keepdims=True) for x in m8]

    def weighted(j):
        vj = vT_ref[0, j]
        return tuple(_mm(vj[h * V_AUG:(h + 1) * V_AUG], _probs(s_ref[h, j], m[h])) for h in pair)

    zero = jnp.zeros((V_AUG, blk), F32)
    res = _grouped_reduce(qi + 1, weighted, both(jnp.add), (zero, zero))
    outs = []
    for h in pair:
        o = res[h][:d] / jnp.maximum(res[h][d:d + 1], 1e-30)
        outs.append(o * lax.rsqrt(jnp.mean(o * o, axis=0, keepdims=True) + NORM_EPS) * gain_ref[0, h])
    o_ref[0] = jnp.concatenate(outs, axis=0).T.astype(o_ref.dtype)


def _moba(qT, kb, vT, gain):
    b, nb, dm, blk = qT.shape
    d2 = 2 * HEAD_DIM
    return pl.pallas_call(
        functools.partial(_moba_body, nb=nb),
        grid=(b, dm // d2, nb),
        in_specs=[pl.BlockSpec((1, 1, d2, blk), lambda i, p, c: (i, c, p, 0)),
                  pl.BlockSpec((1, nb, blk, d2), lambda i, p, c: (i, 0, 0, p)),
                  pl.BlockSpec((1, nb, 2 * V_AUG, blk), lambda i, p, c: (i, 0, p, 0)),
                  pl.BlockSpec((1, 2, HEAD_DIM, 1), lambda i, p, c: (p, 0, 0, 0))],
        out_specs=pl.BlockSpec((1, blk, d2), lambda i, p, c: (i, c, p)),
        out_shape=jax.ShapeDtypeStruct((b, nb * blk, dm), BF16),
        scratch_shapes=[pltpu.VMEM((nb, d2), F32), pltpu.VMEM((2, nb, blk), F32),
                        pltpu.VMEM((2, nb, blk, blk), F32)],
        compiler_params=_params(3), name="moba")(qT, kb, vT, gain.reshape(dm // d2, 2, HEAD_DIM, 1))


def _gelu_tanh(x):
    return x * (0.5 * (1.0 + jnp.tanh(np.sqrt(2.0 / np.pi) * (x + 0.044715 * (x * x * x)))))


def _nsa_cmp_body(r_ref, pos_ref, wtop_ref, wbot_ref, w2_ref, o_ref, oT_ref):
    r = r_ref[0, 0]
    nc = r.shape[0]
    pair = lambda ref: (ref[0, 0], ref[0, 1])
    y = _mm3(r + pos_ref[0, 0], pair(wtop_ref))
    z = _mm3(r + pos_ref[0, 1], pair(wbot_ref))
    pre = y + pltpu.roll(z, nc - 1, axis=0)
    o = _mm3(_gelu_tanh(pre), pair(w2_ref))
    o_ref[0, 0] = o
    oT_ref[0, 0] = o.T


def _nsa_compress(kv, pos, w1, w2):
    _, b, s, _ = kv.shape
    g, d = NSA_KV_HEADS, HEAD_DIM
    nc = s // CMP_STRIDE
    hid = w1.shape[-1]
    half = CMP_LEN // 2
    eye = jnp.eye(g, dtype=F32)
    w1r = w1.reshape(2, 2, half, d, hid)
    w1p = jnp.einsum('thjdc,gk->thjgdkc', w1r, eye).reshape(2, 2, half * g * d, g * hid)
    posp = jnp.broadcast_to(pos.reshape(2, 2, half, 1, d), (2, 2, half, g, d)).reshape(2, 2, 1, half * g * d)
    w2p = jnp.einsum('tcd,gk->tgckd', w2, eye).reshape(2, g * hid, g * d)
    kd = half * g * d
    r = kv.reshape(2, b, nc, kd)
    hilo = lambda w: jnp.stack(_split_bf16(w), axis=1)
    wspec = lambda w: pl.BlockSpec((1,) + w.shape[1:], lambda t, i: (t, 0, 0, 0))
    wtop, wbot, w2p = hilo(w1p[:, 0]), hilo(w1p[:, 1]), hilo(w2p)
    return pl.pallas_call(
        _nsa_cmp_body,
        grid=(2, b),
        in_specs=[pl.BlockSpec((1, 1, nc, kd), lambda t, i: (t, i, 0, 0)),
                  pl.BlockSpec((1, 2, 1, kd), lambda t, i: (t, 0, 0, 0)),
                  wspec(wtop), wspec(wbot), wspec(w2p)],
        out_specs=[pl.BlockSpec((1, 1, nc, g * d), lambda t, i: (t, i, 0, 0)),
                   pl.BlockSpec((1, 1, g * d, nc), lambda t, i: (t, i, 0, 0))],
        out_shape=[jax.ShapeDtypeStruct((2, b, nc, g * d), F32), jax.ShapeDtypeStruct((2, b, g * d, nc), F32)],
        compiler_params=_params(2), name="nsa_compress")(r, posp, wtop, wbot, w2p)


def _cmp_to_slc_T(nc, n_slc):
    r = SLC_BLOCK // CMP_STRIDE
    c = CMP_LEN // CMP_STRIDE
    i = (r * np.arange(n_slc)[:, None, None] - np.arange(r)[None, :, None] - np.arange(c)[None, None, :]).reshape(n_slc, -1)
    m = (i[:, :, None] == np.arange(nc - 1)[None, None, :]).sum(1)
    return np.concatenate([m, np.zeros((n_slc, 1), m.dtype)], axis=1).astype(np.float32)


def _nsa_attn_body(q_ref, gt_ref, kc_ref, vcT_ref, map_ref, ks_ref, vsT_ref, kw_ref, vwT_ref, gain_ref, o_ref,
                   bias_ref, s_ref, *, nc, n_slc):
    grp = pl.program_id(1)
    c = pl.program_id(2)
    tq, rr, d = NSA_TQ, NSA_GROUP, HEAD_DIM
    nl = tq * rr
    t0 = c * tq
    q3 = q_ref[0]
    q = jnp.concatenate([q3[r * d:(r + 1) * d] for r in range(rr)], axis=1)
    qp = _pad_heads(q, grp, NSA_KV_HEADS)
    lane = lax.broadcasted_iota(jnp.int32, (1, nl), 1)
    tpos3 = t0 + (lane & (tq - 1))

    sc = _mm2(kc_ref[0, 0], qp) * SCALE
    n_id = lax.broadcasted_iota(jnp.int32, (nc, nl), 0)
    cmask = (n_id * CMP_STRIDE + (CMP_LEN - 1)) <= tpos3
    scm = jnp.where(cmask, sc, NEG)
    pc = jnp.where(cmask, jnp.exp(scm - jnp.max(scm, axis=0, keepdims=True)), 0.0)
    pc = pc / jnp.maximum(jnp.sum(pc, axis=0, keepdims=True), 1e-30)
    oc = _mm(vcT_ref[0, 0].astype(BF16), pc.astype(BF16))

    pcs = pc[:, 0:tq]
    for r in range(1, rr):
        pcs = pcs + pc[:, r * tq:(r + 1) * tq]
    p_hi, p_lo = _split_bf16(pcs)
    imp = _mm(map_ref[...], p_hi) + _mm(map_ref[...], p_lo)
    bid = lax.broadcasted_iota(jnp.int32, (n_slc, tq), 0)
    tpos = t0 + lax.broadcasted_iota(jnp.int32, (1, tq), 1)
    cur = tpos // SLC_BLOCK
    forced = (bid == 0) | (bid == cur) | (bid == cur - 1)
    val = jnp.where(forced, BIG, jnp.where(bid * SLC_BLOCK <= tpos, imp, -BIG))
    lanes = 128
    halves = [val[:, i * lanes:(i + 1) * lanes] for i in range(tq // lanes)]
    bid_h = bid[:, :lanes]
    for _ in range(min(SLC_TOPK, n_slc)):
        for i, v in enumerate(halves):
            m = jnp.max(v, axis=0, keepdims=True)
            idx = jnp.min(jnp.where(v == m, bid_h, n_slc), axis=0, keepdims=True)
            halves[i] = jnp.where(bid_h == idx, -jnp.inf, v)
    bias_ref[...] = jnp.where(jnp.concatenate(halves, axis=1) == -jnp.inf, 0.0, NEG)

    per_tile = NSA_TK // SLC_BLOCK

    def sel_scores(j, diagonal):
        s = _mm(ks_ref[0, j], qp) * SCALE_LOG2E
        parts = []
        for i in range(per_tile):
            brow = bias_ref[pl.ds(per_tile * j + i, 1), :]
            parts.append(s[i * SLC_BLOCK:(i + 1) * SLC_BLOCK, :] + jnp.concatenate([brow] * rr, axis=1))
        s = jnp.concatenate(parts, axis=0)
        if diagonal:
            kpos = j * NSA_TK + lax.broadcasted_iota(jnp.int32, (NSA_TK, nl), 0)
            s = jnp.where(kpos <= tpos3, s, NEG)
        s_ref[j] = s
        return _fold8(s, jnp.max)

    jl = t0 // NSA_TK
    m8 = _grouped_reduce(jl, lambda j: sel_scores(j, False), jnp.maximum, sel_scores(jl, True))
    m_s = jnp.max(m8, axis=0, keepdims=True)

    acc_s = _grouped_reduce(jl + 1, lambda j: _mm(vsT_ref[0, j], _probs(s_ref[j], m_s)), jnp.add,
                            jnp.zeros((V_AUG, nl), F32))
    o_s = acc_s[:d] / jnp.maximum(acc_s[d:d + 1], 1e-30)

    n_w = WINDOW // tq
    krow = lax.broadcasted_iota(jnp.int32, (tq, nl), 0)
    qcol = lax.broadcasted_iota(jnp.int32, (tq, nl), 1) & (tq - 1)
    w_tiles = []
    for i in range(n_w + 1):
        widx = c - n_w + i
        wcl = jnp.maximum(widx, 0)
        s = _mm(kw_ref[0, wcl], qp) * SCALE_LOG2E
        if i == 0:
            s = jnp.where(krow > qcol, s, NEG)
        if i == n_w:
            s = jnp.where(krow <= qcol, s, NEG)
        else:
            s = s + jnp.where(widx >= 0, 0.0, NEG)
        w_tiles.append((s, wcl))
    m8 = functools.reduce(jnp.maximum, [_fold8(s, jnp.max) for s, _ in w_tiles])
    m_w = jnp.max(m8, axis=0, keepdims=True)
    acc_w = _tree(jnp.add, [_mm(vwT_ref[0, wcl], _probs(s, m_w)) for s, wcl in w_tiles])
    o_w = acc_w[:d] / jnp.maximum(acc_w[d:d + 1], 1e-30)

    g = jax.nn.sigmoid(gt_ref[0])
    outs = []
    for r in range(rr):
        sl = slice(r * tq, (r + 1) * tq)
        o = g[r:r + 1, :] * oc[:, sl] + g[rr + r:rr + r + 1, :] * o_s[:, sl] + g[2 * rr + r:2 * rr + r + 1, :] * o_w[:, sl]
        o = o * lax.rsqrt(jnp.mean(o * o, axis=0, keepdims=True) + NORM_EPS) * gain_ref[0, r]
        outs.append(o)
    outs.append(jnp.zeros((D_NSA_PAD - rr * d, tq), F32))
    o_ref[0] = jnp.concatenate(outs, axis=0).T.astype(o_ref.dtype)


def _nsa(qn, gt, kv, ks, vsT, kw, vwT, cmp_pos, cmp_w1, cmp_w2, gain):
    b, _, s = qn.shape
    g, rr, d, tq = NSA_KV_HEADS, NSA_GROUP, HEAD_DIM, NSA_TQ
    nq, nl = s // tq, NSA_TQ * NSA_GROUP
    nc, n_slc = s // CMP_STRIDE, s // SLC_BLOCK
    cmp, cmpT = _nsa_compress(kv, cmp_pos, cmp_w1, cmp_w2)
    cmap = jnp.asarray(_cmp_to_slc_T(nc, n_slc), BF16)
    whole = lambda arr: pl.BlockSpec((1,) + arr.shape[1:], lambda i, j, c: (i, 0, 0, 0))
    return pl.pallas_call(
        functools.partial(_nsa_attn_body, nc=nc, n_slc=n_slc),
        grid=(b, g, nq),
        in_specs=[pl.BlockSpec((1, rr * d, tq), lambda i, j, c: (i, j, c)),
                  pl.BlockSpec((1, GATE_ROWS, tq), lambda i, j, c: (i, j, c)),
                  pl.BlockSpec((1, 1, nc, g * d), lambda i, j, c: (0, i, 0, 0)),
                  pl.BlockSpec((1, 1, d, nc), lambda i, j, c: (1, i, j, 0)),
                  pl.BlockSpec((n_slc, nc), lambda i, j, c: (0, 0)),
                  whole(ks),
                  pl.BlockSpec((1, s // NSA_TK, V_AUG, NSA_TK), lambda i, j, c: (i, 0, j, 0)),
                  whole(kw),
                  pl.BlockSpec((1, nq, V_AUG, tq), lambda i, j, c: (i, 0, j, 0)),
                  pl.BlockSpec((1, rr, d, 1), lambda i, j, c: (j, 0, 0, 0))],
        out_specs=pl.BlockSpec((1, tq, D_NSA_PAD), lambda i, j, c: (i, c, j)),
        out_shape=jax.ShapeDtypeStruct((b, s, g * D_NSA_PAD), BF16),
        scratch_shapes=[pltpu.VMEM((n_slc, tq), F32), pltpu.VMEM((s // NSA_TK, NSA_TK, nl), F32)],
        compiler_params=_params(3), name="nsa_attn")(qn, gt, cmp, cmpT, cmap, ks, vsT, kw, vwT,
                                                      gain.reshape(g, rr, d, 1))


def _head_ones():
    i = np.arange(D_RWKV) // HEAD_DIM
    return (i[:, None] == i[None, :]).astype(np.float32)


def _rw_prep_body(p_ref, mu_ref, w2_ref, a2_ref, g2_ref, vec_ref, ones_ref,
                  r_ref, lw_ref, k_ref, v_ref, kk_ref, b_ref, g_ref, bonus_ref, carry_ref):
    @pl.when(pl.program_id(1) == 0)
    def _():
        carry_ref[...] = jnp.zeros_like(carry_ref)

    p = p_ref[0]
    t = p.shape[0]
    row = lax.broadcasted_iota(jnp.int32, p.shape, 0)
    prev = jnp.where(row == 0, carry_ref[7:8, :], pltpu.roll(p, 1, axis=0))
    carry_ref[...] = p[t - 8:, :]
    xs = p + (prev - p) * mu_ref[...]
    dr = D_RWKV
    r, k, v, lora = xs[:, :dr], xs[:, dr:2 * dr], xs[:, 2 * dr:3 * dr], xs[:, 3 * dr:]
    w0, a0, k_k, k_a, r_k = (vec_ref[i:i + 1, :] for i in range(5))
    ones = ones_ref[...]
    logw = -RW_DECAY_SCALE * jax.nn.sigmoid(w0 + _mm3(jnp.tanh(lora), w2_ref[...]))
    a = jax.nn.sigmoid(a0 + _mm3(lora, a2_ref[...]))
    g_ref[0] = _mm3(jax.nn.sigmoid(lora), g2_ref[...])
    kk = k * k_k
    kk = kk / jnp.maximum(jnp.sqrt(_mm2(kk * kk, ones)), 1e-12)
    k = k * (1.0 + (a - 1.0) * k_a)
    r_ref[0] = r
    lw_ref[0] = logw
    k_ref[0] = k
    v_ref[0] = v
    kk_ref[0] = kk
    b_ref[0] = kk * a
    bonus_ref[0] = _mm2(r * k * r_k, ones) * v


def _rw_prep(p, mu, w0, w2, a0, a2, g2, k_k, k_a, r_k, tm=512):
    b, s, n = p.shape
    dr = D_RWKV
    nl = n - 3 * dr
    pad = lambda w, lo: jnp.zeros((nl, dr), F32).at[lo:lo + w.shape[0]].set(w)
    w2p, a2p, g2p = pad(w2, 0), pad(a2, w2.shape[0]), pad(g2, w2.shape[0] + a2.shape[0])
    vec = jnp.concatenate([jnp.stack([w0, a0, k_k, k_a, r_k.reshape(dr)]), jnp.zeros((3, dr), F32)])
    full = lambda arr: pl.BlockSpec(arr.shape, lambda i, j: (0, 0))
    ones = jnp.asarray(_head_ones(), BF16)
    mu = mu.reshape(1, n)
    tile = pl.BlockSpec((1, tm, dr), lambda i, j: (i, j, 0))
    return pl.pallas_call(
        _rw_prep_body,
        grid=(b, s // tm),
        in_specs=[pl.BlockSpec((1, tm, n), lambda i, j: (i, j, 0)), full(mu), full(w2p), full(a2p), full(g2p),
                  full(vec), full(ones)],
        out_specs=[tile] * 8,
        out_shape=[jax.ShapeDtypeStruct((b, s, dr), F32)] * 8,
        scratch_shapes=[pltpu.VMEM((8, n), F32)],
        compiler_params=_params(2), name="rwkv_prep")(p, mu, w2p, a2p, g2p, vec, ones)


def _cumsum_rows(x):
    n = x.shape[0]
    row = lax.broadcasted_iota(jnp.int32, x.shape, 0)
    d = 1
    while d < n:
        x = x + jnp.where(row >= d, pltpu.roll(x, d, axis=0), 0.0)
        d *= 2
    return x


def _unit_lower_inverses(ns, row, col):
    eye = (row == col).astype(F32)
    size = ns[0].shape[0]
    n8 = [jnp.where((row >> 3) == (col >> 3), n, 0.0) for n in ns]
    n8s = _each(_split_bf16, n8)
    n8_2 = _each(lambda a: _mm3(a, a), n8s)
    n8_2s = _each(_split_bf16, n8_2)
    n8_4 = _each(lambda a: _mm3(a, a), n8_2s)
    p1 = _each(lambda n, n2, a, a2: eye + n + n2 + _mm3(a, a2), n8, n8_2, n8s, n8_2s)
    t = _each(lambda p, n4: p + _mm3(p, n4), p1, n8_4)
    sh = 4
    while (1 << (sh - 1)) < size:
        off = ((row >> sh) == (col >> sh)) & ((row >> (sh - 1)) != (col >> (sh - 1)))
        ts = _each(_split_bf16, t)
        tc = _each(lambda a, n: _mm3(a, jnp.where(off, n, 0.0)), ts, ns)
        t = _each(lambda x, y, a: x + _mm3(y, a), t, tc, ts)
        sh += 1
    return t


def _rw_scan_body(r_ref, lw_ref, k_ref, v_ref, kk_ref, b_ref, bonus_ref, g_ref, vec_ref, avg_ref, o_ref, h_ref,
                  *, n_chunks):
    @pl.when(pl.program_id(1) == 0)
    def _():
        h_ref[...] = jnp.zeros_like(h_ref)

    cs, d, nh = RW_CHUNK, HEAD_DIM, RWKV_HEADS
    row = lax.broadcasted_iota(jnp.int32, (cs, cs), 0)
    col = lax.broadcasted_iota(jnp.int32, (cs, cs), 1)
    eye = (row == col).astype(F32)
    cat0 = lambda *xs: jnp.concatenate(xs, axis=0)
    cat1 = lambda *xs: jnp.concatenate(xs, axis=1)
    units = [(c, h) for c in range(n_chunks) for h in range(nh)]
    take = lambda ref: [ref[0, c * cs:(c + 1) * cs, :][:, h * d:(h + 1) * d] for c, h in units]
    r, lw, k, v, kk, beta = (take(ref) for ref in (r_ref, lw_ref, k_ref, v_ref, kk_ref, b_ref))

    cum = _each(_cumsum_rows, lw)
    tot = [x[cs - 1:cs, :] for x in cum]
    a_t = _each(lambda kk_, c_, l_: -kk_ * jnp.exp(c_ - l_), kk, cum, lw)
    r_t = _each(lambda r_, c_: r_ * jnp.exp(c_), r, cum)
    b_t = _each(lambda b_, c_: b_ * jnp.exp(-c_), beta, cum)
    k_t = _each(lambda k_, c_: k_ * jnp.exp(-c_), k, cum)
    b_hT = _each(lambda b_, t_, c_: (b_ * jnp.exp(t_ - c_)).T, beta, tot, cum)
    k_hT = _each(lambda k_, t_, c_: (k_ * jnp.exp(t_ - c_)).T, k, tot, cum)
    vs = _each(_split_bf16, v)
    bks = _each(lambda b_, k_: _split_bf16(cat0(b_, k_)), b_t, k_t)
    gram_a = _each(lambda a_, x_: _mm3(a_, x_, _NT), a_t, bks)
    gram_r = _each(lambda r_, x_: lax.dot_general(r_.astype(BF16), x_[0], _NT, preferred_element_type=F32), r_t, bks)
    a_ab = [jnp.where(row > col, x[:, :cs], 0.0) for x in gram_a]
    a_ak = [jnp.where(row > col, x[:, cs:], 0.0) for x in gram_a]
    m_rb = [jnp.where(row >= col, x[:, :cs], 0.0).astype(BF16) for x in gram_r]
    m_rk = [jnp.where(row >= col, x[:, cs:], 0.0).astype(BF16) for x in gram_r]
    t_inv = _unit_lower_inverses(a_ab, row, col)
    akv = _each(_mm3, a_ak, vs)
    rkv = _each(lambda m_, v_: _mm(m_, v_[0]), m_rk, vs)
    khv = _each(_mm3, k_hT, vs)
    wus = _each(lambda t_, a_, x_: _split_bf16(_mm3(t_, cat1(a_, x_))), t_inv, a_t, akv)
    qo = _each(lambda m_, w_, r_, x_: _mm(m_, w_[0]) + cat1(r_, x_), m_rb, wus, r_t, rkv)
    pd = _each(lambda b_, w_, t_, x_: _mm3(b_, w_) + cat1(eye * jnp.exp(t_), x_), b_hT, wus, tot, khv)

    hs = [h_ref[h] for h in range(nh)]
    outs = []
    for c in range(n_chunks):
        heads = []
        for h in range(nh):
            u = c * nh + h
            hsplit = _split_bf16(hs[h])
            heads.append(_mm(qo[u][:, :cs].astype(BF16), hsplit[0]) + qo[u][:, cs:])
            hs[h] = _mm3(pd[u][:, :cs], hsplit) + pd[u][:, cs:]
        outs.append(cat1(*heads))
    for h in range(nh):
        h_ref[h] = hs[h]

    o = cat0(*outs)
    avg = avg_ref[...]
    mean = _mm2(o, avg)
    ctr = o - mean
    var = _mm2(ctr * ctr, avg)
    y = ctr * lax.rsqrt(var + RW_LN_EPS) * vec_ref[0:1, :] + vec_ref[1:2, :]
    o_ref[0] = ((y + bonus_ref[0]) * g_ref[0]).astype(o_ref.dtype)


def _rwkv(p, mu, w0, w2, a0, a2, g2, k_k, k_a, r_k, lnx_w, lnx_b, tb=512):
    b, s, _ = p.shape
    dr = D_RWKV
    r, lw, k, v, kk, beta, g, bonus = _rw_prep(p, mu, w0, w2, a0, a2, g2, k_k, k_a, r_k)
    vec = jnp.concatenate([jnp.stack([lnx_w, lnx_b]), jnp.zeros((6, dr), F32)])
    avg = jnp.asarray(_head_ones() / HEAD_DIM, BF16)
    tile = pl.BlockSpec((1, tb, dr), lambda i, c: (i, c, 0))
    full = lambda arr: pl.BlockSpec(arr.shape, lambda i, c: (0, 0))
    return pl.pallas_call(
        functools.partial(_rw_scan_body, n_chunks=tb // RW_CHUNK),
        grid=(b, s // tb),
        in_specs=[tile] * 8 + [full(vec), full(avg)],
        out_specs=tile,
        out_shape=jax.ShapeDtypeStruct((b, s, dr), BF16),
        scratch_shapes=[pltpu.VMEM((RWKV_HEADS, HEAD_DIM, HEAD_DIM), F32)],
        compiler_params=_params(2), name="rwkv_scan")(r, lw, k, v, kk, beta, bonus, g, vec, avg)


def kernel(x, attn_norm, w_in, nsa_cmp_pos, nsa_cmp_w1, nsa_cmp_w2, nsa_out_gain, rw_mu, rw_w0, rw_w2, rw_a0, rw_a2, rw_g2, rw_k_k, rw_k_a, rw_r_k, rw_lnx_w, rw_lnx_b, moba_out_gain, w_out, ffn_norm, ffn_w_in, ffn_conv_w, ffn_conv_b, ffn_w_out, final_norm):
    b, s, d = x.shape
    depth = w_in.shape[0]
    d_ff = ffn_w_out.shape[1]
    g3 = NSA_GROUP * HEAD_DIM
    w_nsa = w_out[:, :D_NSA].reshape(depth, NSA_KV_HEADS, g3, d)
    w_nsa = jnp.pad(w_nsa, ((0, 0), (0, 0), (0, D_NSA_PAD - g3), (0, 0))).reshape(depth, NSA_KV_HEADS * D_NSA_PAD, d)
    w_nsa = w_nsa.astype(BF16)
    w_rw = w_out[:, D_NSA:D_NSA + D_RWKV].astype(BF16)
    w_moba = w_out[:, D_NSA + D_RWKV:].astype(BF16)
    ffn_w_in_b = ffn_w_in.astype(BF16)
    ffn_w_out_b = ffn_w_out.astype(BF16)
    for l in range(depth):
        wn, wt = _proj_weights(w_in[l])
        kv, ks, kw, rw, mk, qn, vs, vw, gt, mq, mv = _norm_proj(x, attn_norm[l], wn, wt)
        o_nsa = _nsa(qn, gt, kv, ks, vs, kw, vw, nsa_cmp_pos[l], nsa_cmp_w1[l], nsa_cmp_w2[l], nsa_out_gain[l])
        o_rw = _rwkv(rw, rw_mu[l], rw_w0[l], rw_w2[l], rw_a0[l], rw_a2[l], rw_g2[l], rw_k_k[l], rw_k_a[l],
                     rw_r_k[l], rw_lnx_w[l], rw_lnx_b[l])
        o_moba = _moba(mq, mk, mv, moba_out_gain[l])
        x = _out_proj(x.reshape(b * s, d), o_nsa.reshape(b * s, -1), o_rw.reshape(b * s, D_RWKV),
                      o_moba.reshape(b * s, D_MOBA), w_nsa[l], w_rw[l], w_moba[l]).reshape(b, s, d)
        x = _ffn(x, ffn_norm[l], ffn_w_in_b[l, :, :d_ff], ffn_w_in_b[l, :, d_ff:], ffn_conv_w[l], ffn_conv_b[l],
                 ffn_w_out_b[l], final_norm, final_norm=(l == depth - 1))
    return x
```
